```python
import math
import jax
import jax.numpy as jnp
from jax import lax
import numpy as np

D_MODEL = 1024
BATCH = 8
SEQ = 2048
DEPTH = 1

MEM_LEN = 256
NSA_HEAD_DIM = 64
NSA_WIDTH = D_MODEL // 2
NSA_HEADS = NSA_WIDTH // NSA_HEAD_DIM
NSA_KV_GROUPS = 2
NSA_Q_PER_KV = NSA_HEADS // NSA_KV_GROUPS
CMP_BLOCK = 32
CMP_STRIDE = 16
CMP_HIDDEN = 2 * NSA_HEAD_DIM
SLC_BLOCK = 64
SLC_TOPN = 16
SLC_QBLOCK = 64
WIN = 512
WIN_QBLOCK = 128
FORCE_SCORE = 1e4
HG_DK = 128
HG_DV = HG_DK
HG_HEADS = (D_MODEL - NSA_WIDTH) // HG_DK
HG_CHUNK = 64
XA_HEADS = 4
XA_HEAD_DIM = D_MODEL // XA_HEADS
PEER_HEADS = 8
PEER_NKEYS = 128
PEER_N_EXPERTS = PEER_NKEYS * PEER_NKEYS
PEER_DKEY = 128
PEER_TOPK = 16
PEER_TOKEN_BLOCK = 128
IN_SPLITS = (NSA_WIDTH,) + (NSA_KV_GROUPS * NSA_HEAD_DIM,) * 6 + (NSA_HEADS * 3,) + (HG_HEADS * HG_DK, HG_HEADS * HG_DK, HG_HEADS * HG_DV, HG_HEADS * HG_DV)
IN_COLS = sum(IN_SPLITS)
LN_EPS = 1e-5
NEG_INF = -1e30
DEEPNORM_ALPHA = (2.0 * DEPTH) ** 0.25
DEEPNORM_BETA = (8.0 * DEPTH) ** -0.25

kernel_name = 'nsa_hgrn2_peer_deepnorm_hybrid'


def _layer_norm(x, g, b):
    xf = x.astype(jnp.float32)
    mu = jnp.mean(xf, axis=-1, keepdims=True)
    var = jnp.mean(jnp.square(xf - mu), axis=-1, keepdims=True)
    y = (xf - mu) * lax.rsqrt(var + LN_EPS)
    return (y * g.astype(jnp.float32) + b.astype(jnp.float32)).astype(x.dtype)


def _masked_softmax(s, mask):
    return jax.nn.softmax(jnp.where(mask, s, NEG_INF), axis=-1)


def _compress(blocks, pos_emb, w1, w2):
    b, nc, l, g, dh = blocks.shape
    h = (blocks + pos_emb[None, None, :, None, :]).transpose(0, 1, 3, 2, 4).reshape(b, nc, g, l * dh)
    return jax.nn.gelu(h @ w1) @ w2


def _nsa(q, k_cmp, v_cmp, k_slc, v_slc, k_win, v_win, gate_logits, cmp_pos, cmp_w1, cmp_w2):
    B, S = q.shape[0], q.shape[1]
    G, R, Dh = NSA_KV_GROUPS, NSA_Q_PER_KV, NSA_HEAD_DIM
    scale = Dh ** -0.5
    qg = q.reshape(B, S, G, R, Dh)
    pos = np.arange(S)

    n_cmp = (S - CMP_BLOCK) // CMP_STRIDE + 1
    cmp_idx = np.arange(n_cmp)[:, None] * CMP_STRIDE + np.arange(CMP_BLOCK)[None, :]
    kc = _compress(k_cmp[:, cmp_idx], cmp_pos[0], cmp_w1[0], cmp_w2[0])
    vc = _compress(v_cmp[:, cmp_idx], cmp_pos[1], cmp_w1[1], cmp_w2[1])
    cmp_valid = cmp_idx[:, -1][None, :] <= pos[:, None]
    s_cmp = jnp.einsum('bsgrd,bngd->bgrsn', qg, kc).astype(jnp.float32) * scale
    p_cmp = _masked_softmax(s_cmp, cmp_valid) * cmp_valid.any(-1)[:, None]
    o_cmp = jnp.einsum('bgrsn,bngd->bsgrd', p_cmp.astype(vc.dtype), vc)

    n_slc = S // SLC_BLOCK
    n_top = min(SLC_TOPN, n_slc)
    slc_start = np.arange(n_slc) * SLC_BLOCK
    cmp_start = np.arange(n_cmp) * CMP_STRIDE
    overlap = ((cmp_start[:, None] < slc_start[None, :] + SLC_BLOCK)
               & (cmp_start[:, None] + CMP_BLOCK > slc_start[None, :])).astype(np.float32)
    imp = jnp.einsum('bgrsn,nj->bsgj', p_cmp, jnp.asarray(overlap))
    cur = pos // SLC_BLOCK
    jb = np.arange(n_slc)
    forced = (jb[None] == 0) | (jb[None] == cur[:, None]) | (jb[None] == cur[:, None] - 1)
    future = jb[None] > cur[:, None]
    sel_score = jnp.where(future[None, :, None, :], -jnp.inf,
                          jnp.where(forced[None, :, None, :], FORCE_SCORE, imp))
    _, sel_idx = lax.top_k(sel_score, n_top)

    ks_b = k_slc.reshape(B, n_slc, SLC_BLOCK, G, Dh).transpose(0, 3, 1, 2, 4)
    vs_b = v_slc.reshape(B, n_slc, SLC_BLOCK, G, Dh).transpose(0, 3, 1, 2, 4)
    nqb = S // SLC_QBLOCK
    q_blocks = qg.reshape(B, nqb, SLC_QBLOCK, G, R, Dh).swapaxes(0, 1)
    i_blocks = sel_idx.reshape(B, nqb, SLC_QBLOCK, G, n_top).swapaxes(0, 1)
    p_blocks = jnp.asarray(pos.reshape(nqb, SLC_QBLOCK))
    b_ix = jnp.arange(B)[:, None, None, None]
    g_ix = jnp.arange(G)[None, None, :, None]

    def slc_block(args):
        qb, ib, pb = args
        kg = ks_b[b_ix, g_ix, ib]
        vg = vs_b[b_ix, g_ix, ib]
        s = jnp.einsum('bqgrd,bqgnld->bqgrnl', qb, kg).astype(jnp.float32) * scale
        key_pos = ib[..., None] * SLC_BLOCK + jnp.arange(SLC_BLOCK)
        ok = (key_pos <= pb[None, :, None, None, None])[:, :, :, None]
        s = jnp.where(ok, s, NEG_INF)
        p = jax.nn.softmax(s.reshape(B, SLC_QBLOCK, G, R, n_top * SLC_BLOCK), axis=-1).reshape(s.shape)
        return jnp.einsum('bqgrnl,bqgnld->bqgrd', p.astype(vg.dtype), vg)

    o_slc = lax.map(slc_block, (q_blocks, i_blocks, p_blocks)).swapaxes(0, 1).reshape(B, S, G, R, Dh)

    nwb = S // WIN_QBLOCK
    band = np.arange(nwb)[:, None] * WIN_QBLOCK + np.arange(WIN + WIN_QBLOCK)[None, :]
    key_pos_w = band - WIN
    q_pos_w = pos.reshape(nwb, WIN_QBLOCK)
    win_ok = ((key_pos_w[:, None, :] <= q_pos_w[:, :, None])
              & (key_pos_w[:, None, :] > q_pos_w[:, :, None] - WIN)
              & (key_pos_w[:, None, :] >= 0))
    pad = ((0, 0), (WIN, 0), (0, 0), (0, 0))
    kband = jnp.pad(k_win, pad)[:, band].swapaxes(0, 1)
    vband = jnp.pad(v_win, pad)[:, band].swapaxes(0, 1)
    qw = qg.reshape(B, nwb, WIN_QBLOCK, G, R, Dh).swapaxes(0, 1)

    def win_block(args):
        qb, kb, vb, mb = args
        s = jnp.einsum('bqgrd,bkgd->bgrqk', qb, kb).astype(jnp.float32) * scale
        p = _masked_softmax(s, mb[None, None, None])
        return jnp.einsum('bgrqk,bkgd->bqgrd', p.astype(vb.dtype), vb)

    o_win = lax.map(win_block, (qw, kband, vband, jnp.asarray(win_ok))).swapaxes(0, 1).reshape(B, S, G, R, Dh)

    gates = jax.nn.sigmoid(gate_logits.astype(jnp.float32)).reshape(B, S, G, R, 3).astype(q.dtype)
    o = gates[..., 0:1] * o_cmp + gates[..., 1:2] * o_slc + gates[..., 2:3] * o_win
    return o.reshape(B, S, NSA_WIDTH)


def _hgrn2(q, f_logits, i, g, lower_bound, norm_w):
    B, S = q.shape[0], q.shape[1]
    H, Dk, Dv, C = HG_HEADS, HG_DK, HG_DV, HG_CHUNK
    nc = S // C
    lb = lower_bound[None, None]
    f = lb + (1.0 - lb) * jax.nn.sigmoid(f_logits.astype(jnp.float32).reshape(B, S, H, Dk))
    k = 1.0 - f
    log_f = jnp.log(f)

    def chunks(a, d):
        return a.astype(jnp.float32).reshape(B, nc, C, H, d).transpose(1, 0, 3, 2, 4)

    causal = np.tril(np.ones((C, C), dtype=bool))[None, None, :, :, None]

    def step(state, inp):
        qc, kc, vc, gc = inp
        b = jnp.cumsum(gc, axis=2)
        decay = jnp.exp(jnp.where(causal, b[:, :, :, None, :] - b[:, :, None, :, :], -jnp.inf))
        attn = jnp.einsum('bhtd,bhsd,bhtsd->bhts', qc, kc, decay)
        o = jnp.einsum('bhts,bhse->bhte', attn, vc) + jnp.einsum('bhtd,bhde->bhte', qc * jnp.exp(b), state)
        b_last = b[:, :, -1:, :]
        new_state = (jnp.exp(b_last[:, :, 0, :])[..., None] * state
                     + jnp.einsum('bhsd,bhse->bhde', kc * jnp.exp(b_last - b), vc))
        return new_state, o

    state0 = jnp.zeros((B, H, Dk, Dv), jnp.float32)
    _, o = lax.scan(step, state0, (chunks(q, Dk), chunks(k, Dk), chunks(i, Dv), chunks(log_f, Dk)))
    o = o.transpose(1, 0, 3, 2, 4).reshape(B, S, H, Dv)
    o = o * lax.rsqrt(jnp.mean(jnp.square(o), axis=-1, keepdims=True) + LN_EPS) * norm_w.astype(jnp.float32)
    o = o * jax.nn.silu(g.astype(jnp.float32).reshape(B, S, H, Dv))
    return o.reshape(B, S, H * Dv).astype(q.dtype)


def _hybrid_mixer(x, w_in, cmp_pos, cmp_w1, cmp_w2, lower_bound, norm_w, w_out):
    B, S, _ = x.shape
    h = x @ w_in
    offs = [int(o) for o in np.cumsum(IN_SPLITS)[:-1]]
    q_a, kc, vc, ks, vs, kw, vw, gate_a, q_b, f_b, i_b, g_b = jnp.split(h, offs, axis=-1)

    def kv(a):
        return a.reshape(B, S, NSA_KV_GROUPS, NSA_HEAD_DIM)

    o_a = _nsa(q_a.reshape(B, S, NSA_HEADS, NSA_HEAD_DIM), kv(kc), kv(vc), kv(ks), kv(vs), kv(kw), kv(vw),
               gate_a, cmp_pos, cmp_w1, cmp_w2)
    o_b = _hgrn2(q_b, f_b, i_b, g_b, lower_bound, norm_w)
    return jnp.concatenate([o_a, o_b], axis=-1) @ w_out


def _memory_cross_attention(x, mem, wq, wkv, wo):
    B, S, _ = x.shape
    M = mem.shape[1]
    q = (x @ wq).reshape(B, S, XA_HEADS, XA_HEAD_DIM)
    kvm = (mem @ wkv).reshape(B, M, 2, XA_HEADS, XA_HEAD_DIM)
    k, v = kvm[:, :, 0], kvm[:, :, 1]
    s = jnp.einsum('bshd,bmhd->bhsm', q, k).astype(jnp.float32) * XA_HEAD_DIM ** -0.5
    p = jax.nn.softmax(s, axis=-1)
    o = jnp.einsum('bhsm,bmhd->bshd', p.astype(v.dtype), v).reshape(B, S, XA_HEADS * XA_HEAD_DIM)
    return o @ wo


def _peer(x, wq, sub_keys, down, up):
    B, S, D = x.shape
    T = B * S
    t = x.reshape(T, D)
    qry = (t @ wq).reshape(T, PEER_HEADS, 2, PEER_DKEY // 2)
    s = jnp.einsum('thcd,hcnd->thcn', qry, sub_keys).astype(jnp.float32)
    s1, i1 = lax.top_k(s[:, :, 0], PEER_TOPK)
    s2, i2 = lax.top_k(s[:, :, 1], PEER_TOPK)
    cand_s = (s1[..., :, None] + s2[..., None, :]).reshape(T, PEER_HEADS, PEER_TOPK * PEER_TOPK)
    cand_i = (i1[..., :, None] * PEER_NKEYS + i2[..., None, :]).reshape(T, PEER_HEADS, PEER_TOPK * PEER_TOPK)
    top_s, top_pos = lax.top_k(cand_s, PEER_TOPK)
    expert = jnp.take_along_axis(cand_i, top_pos, axis=-1)
    gate = jax.nn.softmax(top_s, axis=-1)
    nb = T // PEER_TOKEN_BLOCK
    E = PEER_HEADS * PEER_TOPK

    def block(args):
        tb, eb, gb = args
        a = jax.nn.gelu(jnp.einsum('ted,td->te', down[eb], tb).astype(jnp.float32), approximate=False)
        return jnp.einsum('te,ted->td', (a * gb).astype(up.dtype), up[eb])

    out = lax.map(block, (t.reshape(nb, PEER_TOKEN_BLOCK, D), expert.reshape(nb, PEER_TOKEN_BLOCK, E),
                          gate.reshape(nb, PEER_TOKEN_BLOCK, E)))
    return out.reshape(B, S, D)


def setup_inputs(seed: int = 0) -> dict:
    key = jax.random.key(seed)
    ks = jax.random.split(key, 18)

    def nrm(k, shape, scale):
        return jax.random.normal(k, shape, jnp.float32) * scale

    D = D_MODEL
    return {
        'x': nrm(ks[0], (BATCH, SEQ, D), 1.0),
        'mem': nrm(ks[1], (BATCH, MEM_LEN, D), 1.0),
        'w_in': nrm(ks[2], (DEPTH, D, IN_COLS), D ** -0.5),
        'cmp_pos': nrm(ks[3], (DEPTH, 2, CMP_BLOCK, NSA_HEAD_DIM), 0.1),
        'cmp_w1': nrm(ks[4], (DEPTH, 2, CMP_BLOCK * NSA_HEAD_DIM, CMP_HIDDEN), (CMP_BLOCK * NSA_HEAD_DIM) ** -0.5),
        'cmp_w2': nrm(ks[5], (DEPTH, 2, CMP_HIDDEN, NSA_HEAD_DIM), CMP_HIDDEN ** -0.5),
        'hgrn_lb_logits': nrm(ks[6], (DEPTH + 1, HG_HEADS, HG_DK), 0.5),
        'hgrn_norm_w': 1.0 + nrm(ks[7], (DEPTH, HG_HEADS, HG_DV), 0.02),
        'w_out': nrm(ks[8], (DEPTH, D, D), D ** -0.5 * DEEPNORM_BETA),
        'xa_wq': nrm(ks[9], (DEPTH, D, D), D ** -0.5),
        'xa_wkv': nrm(ks[10], (DEPTH, D, 2 * D), D ** -0.5),
        'xa_wo': nrm(ks[11], (DEPTH, D, D), D ** -0.5 * DEEPNORM_BETA),
        'peer_wq': nrm(ks[12], (DEPTH, D, PEER_HEADS * PEER_DKEY), D ** -0.5),
        'peer_subkeys': nrm(ks[13], (DEPTH, PEER_HEADS, 2, PEER_NKEYS, PEER_DKEY // 2), (PEER_DKEY // 2) ** -0.5),
        'peer_down': nrm(ks[14], (DEPTH, PEER_N_EXPERTS, D), D ** -0.5),
        'peer_up': nrm(ks[15], (DEPTH, PEER_N_EXPERTS, D), DEEPNORM_BETA),
        'ln_g': 1.0 + nrm(ks[16], (DEPTH, 3, D), 0.02),
        'ln_b': nrm(ks[17], (DEPTH, 3, D), 0.02),
    }


def reference(x, mem, w_in, cmp_pos, cmp_w1, cmp_w2, hgrn_lb_logits, hgrn_norm_w, w_out,
              xa_wq, xa_wkv, xa_wo, peer_wq, peer_subkeys, peer_down, peer_up, ln_g, ln_b):
    lower_bounds = jnp.cumsum(jax.nn.softmax(hgrn_lb_logits.astype(jnp.float32), axis=0), axis=0)
    for l in range(DEPTH):
        mix = _hybrid_mixer(x, w_in[l], cmp_pos[l], cmp_w1[l], cmp_w2[l], lower_bounds[l], hgrn_norm_w[l], w_out[l])
        x = _layer_norm(DEEPNORM_ALPHA * x + mix, ln_g[l, 0], ln_b[l, 0])
        xa = _memory_cross_attention(x, mem, xa_wq[l], xa_wkv[l], xa_wo[l])
        x = _layer_norm(DEEPNORM_ALPHA * x + xa, ln_g[l, 1], ln_b[l, 1])
        ff = _peer(x, peer_wq[l], peer_subkeys[l], peer_down[l], peer_up[l])
        x = _layer_norm(DEEPNORM_ALPHA * x + ff, ln_g[l, 2], ln_b[l, 2])
    return x
```

```python
import functools
import math

import jax
import jax.numpy as jnp
import numpy as np
from jax import lax
from jax.experimental import pallas as pl
from jax.experimental.pallas import tpu as pltpu

F32 = jnp.float32
BF16 = jnp.bfloat16
I32 = jnp.int32

HEAD_DIM = 64
KV_GROUPS = 2
Q_PER_KV = 4
CMP_BLOCK = 32
CMP_STRIDE = 16
SLC_BLOCK = 64
SLC_TOPN = 16
WIN = 512
FORCE_SCORE = 1e4
HG_HEADS = 4
HG_D = 128
HG_CHUNK = 64
XA_HEADS = 4
PEER_HEADS = 8
PEER_NKEYS = 128
PEER_TOPK = 16
LN_EPS = 1e-5
NEG_INF = -1e30
DEPTH = 1
ALPHA = (2.0 * DEPTH) ** 0.25

LANES = 128
SUBLANES = 8
VMEM_LIMIT = 56 * 1024 * 1024

COL_Q = 0
COL_KV = 512
COL_GATE = 1280
COL_HG = 1536
PACKED_COLS = 3584

GM_TILE = 128
GM_PITCH = 136


def _dot(a, b):
    return jnp.dot(a, b, preferred_element_type=F32)


def _dot_nt(a, b):
    return lax.dot_general(a, b, (((1,), (1,)), ((), ())), preferred_element_type=F32)


def _dot_tn(a, b):
    return lax.dot_general(a, b, (((0,), (0,)), ((), ())), preferred_element_type=F32)


def _layer_norm(y, g, b):
    mu = jnp.mean(y, axis=-1, keepdims=True)
    d = y - mu
    var = jnp.mean(d * d, axis=-1, keepdims=True)
    return d * lax.rsqrt(var + LN_EPS) * g + b


def _cparams(sem):
    return pltpu.CompilerParams(dimension_semantics=sem, vmem_limit_bytes=VMEM_LIMIT)


def _matmul_kernel(x_ref, w_ref, o_ref):
    o_ref[...] = _dot(x_ref[...].astype(BF16), w_ref[...]).astype(o_ref.dtype)


def _matmul(x, w, out_dtype, tm):
    m, k = x.shape
    n = w.shape[1]
    return pl.pallas_call(
        _matmul_kernel,
        grid=(m // tm,),
        in_specs=[pl.BlockSpec((tm, k), lambda i: (i, 0)), pl.BlockSpec((k, n), lambda i: (0, 0))],
        out_specs=pl.BlockSpec((tm, n), lambda i: (i, 0)),
        out_shape=jax.ShapeDtypeStruct((m, n), out_dtype),
        compiler_params=_cparams(("arbitrary",)),
        name="matmul",
    )(x, w)


NSA_TQ = 128
NSA_TK = 256
NSA_TKW = 128


def _online_softmax_step(qs, k_tile, v_tile, ok, carry, scale):
    m, l, acc = carry
    s = _dot_nt(qs, k_tile) * scale
    s = jnp.where(ok, s, NEG_INF)
    m_new = jnp.maximum(m, jnp.max(s, axis=-1, keepdims=True))
    alpha = jnp.exp(m - m_new)
    p = jnp.where(ok, jnp.exp(s - m_new), 0.0)
    l = alpha * l + jnp.sum(p, axis=-1, keepdims=True)
    acc = alpha * acc + _dot(p.astype(BF16), v_tile)
    return m_new, l, acc


def _nsa_kernel(q_ref, kc_ref, vc_ref, ks_ref, vs_ref, kw_ref, vw_ref, gt_ref, w1_ref, pos_ref, w2_ref, ov_ref,
                e_ref, o_ref, kcmp_s, vcmp_s, ks_s, vs_s, kw_s, vw_s, *, seq, n_top):
    g = pl.program_id(1)
    qi = pl.program_id(2)
    tq, dh, r_heads = NSA_TQ, HEAD_DIM, Q_PER_KV
    rows = r_heads * tq
    scale = dh ** -0.5
    n_cmp_pad = seq // CMP_STRIDE

    @pl.when(qi == 0)
    def _prepare():
        def pick(ref):
            xx = ref[...]
            return jnp.where(g == 0, xx[:, :dh], xx[:, dh:]).astype(BF16)

        ks_s[...] = pick(ks_ref)
        vs_s[...] = pick(vs_ref)
        kw_s[...] = pick(kw_ref)
        vw_s[...] = pick(vw_ref)
        half = CMP_BLOCK // 2
        for kv, (src, dst) in enumerate(((kc_ref, kcmp_s), (vc_ref, vcmp_s))):
            ya = jnp.zeros((n_cmp_pad, CMP_BLOCK * 4), F32)
            yb = jnp.zeros((n_cmp_pad, CMP_BLOCK * 4), F32)
            for l in range(half):
                xl = src[pl.ds(l, n_cmp_pad, stride=CMP_STRIDE), :]
                ya = ya + _dot((xl + pos_ref[kv, l:l + 1, :]).astype(BF16), w1_ref[kv, l])
                yb = yb + _dot((xl + pos_ref[kv, half + l:half + l + 1, :]).astype(BF16), w1_ref[kv, half + l])
            h1 = ya + pltpu.roll(yb, n_cmp_pad - 1, 0)
            h1 = jax.nn.gelu(h1, approximate=True)
            dst[...] = _dot(h1.astype(BF16), w2_ref[kv]).astype(BF16)

    q0 = qi * tq
    q4 = q_ref[...]
    qs = jnp.concatenate([q4[:, r * dh:(r + 1) * dh] for r in range(r_heads)], axis=0).astype(BF16)
    pos = q0 + (lax.broadcasted_iota(I32, (rows, 1), 0) & (tq - 1))

    s = _dot_nt(qs, kcmp_s[...]) * scale
    n_idx = lax.broadcasted_iota(I32, (1, n_cmp_pad), 1)
    valid = (n_idx * CMP_STRIDE + (CMP_BLOCK - 1)) <= pos
    s = jnp.where(valid, s, NEG_INF)
    s = s - jnp.max(s, axis=-1, keepdims=True)
    p = jnp.exp(s)
    p = p / jnp.sum(p, axis=-1, keepdims=True)
    p = jnp.where(pos >= CMP_BLOCK - 1, p, 0.0).astype(BF16)
    o_cmp = _dot(p, vcmp_s[...])
    imp4 = _dot(p, ov_ref[...])
    imp = imp4[0:tq]
    for r in range(1, r_heads):
        imp = imp + imp4[r * tq:(r + 1) * tq]

    n_slc = seq // SLC_BLOCK
    sc = imp.T[0:n_slc, :]
    j_idx = lax.broadcasted_iota(I32, (n_slc, 1), 0)
    cur = (q0 + lax.broadcasted_iota(I32, (1, tq), 1)) // SLC_BLOCK
    forced = (j_idx == 0) | (j_idx == cur) | (j_idx == cur - 1)
    score = jnp.where(j_idx > cur, -jnp.inf, jnp.where(forced, FORCE_SCORE, sc))
    cnt = jnp.zeros((n_slc, tq), F32)
    for i in range(n_slc):
        ri = score[i:i + 1, :]
        beats = (ri > score) | ((ri == score) & (j_idx > i))
        cnt = cnt + jnp.where(beats, 1.0, 0.0)
    sel_t = jnp.where(cnt < n_top, 1.0, 0.0)
    sel_t = jnp.concatenate([sel_t, jnp.zeros((LANES - n_slc, tq), F32)], axis=0)
    sel = sel_t.T.astype(BF16)

    def slc_body(kt, carry):
        k0 = pl.multiple_of(kt * NSA_TK, NSA_TK)
        mf = _dot(sel, e_ref[kt])
        mf = jnp.concatenate([mf] * r_heads, axis=0)
        kpos = k0 + lax.broadcasted_iota(I32, (1, NSA_TK), 1)
        ok = (mf > 0.5) & (kpos <= pos)
        return _online_softmax_step(qs, ks_s[pl.ds(k0, NSA_TK), :], vs_s[pl.ds(k0, NSA_TK), :], ok, carry, scale)

    init = (jnp.full((rows, 1), NEG_INF, F32), jnp.zeros((rows, 1), F32), jnp.zeros((rows, dh), F32))
    n_kt = (q0 + tq + NSA_TK - 1) // NSA_TK
    _, l_s, acc_s = lax.fori_loop(0, n_kt, slc_body, init)
    o_slc = acc_s / l_s

    def win_body(i, carry):
        k0 = pl.multiple_of(q0 - WIN + i * NSA_TKW, NSA_TKW)
        kpos = k0 + lax.broadcasted_iota(I32, (1, NSA_TKW), 1)
        ok = (kpos <= pos) & (kpos > pos - WIN)
        return _online_softmax_step(qs, kw_s[pl.ds(k0, NSA_TKW), :], vw_s[pl.ds(k0, NSA_TKW), :], ok, carry, scale)

    n_wt = (WIN + tq) // NSA_TKW
    first = jnp.maximum(0, (WIN - q0) // NSA_TKW)
    _, l_w, acc_w = lax.fori_loop(first, n_wt, win_body, init)
    o_win = acc_w / l_w

    gate = jax.nn.sigmoid(gt_ref[...])
    outs = []
    for r in range(r_heads):
        sl = slice(r * tq, (r + 1) * tq)
        outs.append(gate[:, 3 * r:3 * r + 1] * o_cmp[sl] + gate[:, 3 * r + 1:3 * r + 2] * o_slc[sl]
                    + gate[:, 3 * r + 2:3 * r + 3] * o_win[sl])
    o_ref[...] = jnp.concatenate(outs, axis=1).astype(o_ref.dtype)


def _nsa_constants(seq):
    n_cmp = (seq - CMP_BLOCK) // CMP_STRIDE + 1
    n_slc = seq // SLC_BLOCK
    cmp_start = np.arange(n_cmp) * CMP_STRIDE
    slc_start = np.arange(n_slc) * SLC_BLOCK
    overlap = ((cmp_start[:, None] < slc_start[None, :] + SLC_BLOCK)
               & (cmp_start[:, None] + CMP_BLOCK > slc_start[None, :])).astype(np.float32)
    ov = np.zeros((seq // CMP_STRIDE, LANES), np.float32)
    ov[:n_cmp, :n_slc] = overlap
    key = np.arange(seq)
    expand = (key[None, :] // SLC_BLOCK == np.arange(LANES)[:, None]).astype(np.float32)
    expand = expand.reshape(LANES, seq // NSA_TK, NSA_TK).transpose(1, 0, 2)
    return jnp.asarray(ov, BF16), jnp.asarray(expand, BF16)


def _nsa(h, w1pad, pos2, w2, *, batch, seq):
    t = h.shape[0]
    nq = seq // NSA_TQ
    ov, expand = _nsa_constants(seq)
    n_top = min(SLC_TOPN, seq // SLC_BLOCK)
    kv_spec = lambda c: pl.BlockSpec((seq, LANES), lambda b, g, qi, c=c: (b, c))
    kvb = COL_KV // LANES
    in_specs = [
        pl.BlockSpec((NSA_TQ, Q_PER_KV * HEAD_DIM), lambda b, g, qi: (b * nq + qi, g)),
        kv_spec(kvb), kv_spec(kvb + 1), kv_spec(kvb + 2), kv_spec(kvb + 3), kv_spec(kvb + 4), kv_spec(kvb + 5),
        pl.BlockSpec((NSA_TQ, LANES), lambda b, g, qi: (b * nq + qi, COL_GATE // LANES + g)),
        pl.BlockSpec((2, None, CMP_BLOCK, LANES, LANES), lambda b, g, qi: (0, g, 0, 0, 0)),
        pl.BlockSpec(pos2.shape, lambda b, g, qi: (0, 0, 0)),
        pl.BlockSpec(w2.shape, lambda b, g, qi: (0, 0, 0)),
        pl.BlockSpec(ov.shape, lambda b, g, qi: (0, 0)),
        pl.BlockSpec(expand.shape, lambda b, g, qi: (0, 0, 0)),
    ]
    n_cmp_pad = seq // CMP_STRIDE
    scratch = [pltpu.VMEM((n_cmp_pad, HEAD_DIM), BF16), pltpu.VMEM((n_cmp_pad, HEAD_DIM), BF16)] + [
        pltpu.VMEM((seq, HEAD_DIM), BF16) for _ in range(4)]
    return pl.pallas_call(
        functools.partial(_nsa_kernel, seq=seq, n_top=n_top),
        grid=(batch, KV_GROUPS, nq),
        in_specs=in_specs,
        out_specs=pl.BlockSpec((NSA_TQ, Q_PER_KV * HEAD_DIM), lambda b, g, qi: (b * nq + qi, g)),
        out_shape=jax.ShapeDtypeStruct((t, KV_GROUPS * Q_PER_KV * HEAD_DIM), BF16),
        scratch_shapes=scratch,
        compiler_params=_cparams(("arbitrary", "arbitrary", "arbitrary")),
        name="nsa",
    )(h, h, h, h, h, h, h, h, w1pad, pos2, w2, ov, expand)


HG_CHUNKS_PER_STEP = 2


def _boundary_rows(b_ref, m):
    blk = 2 * m
    pieces = []
    sub = lax.broadcasted_iota(I32, (SUBLANES, 1), 0)
    for v in range(HG_CHUNK // SUBLANES):
        if blk >= SUBLANES:
            r = (SUBLANES * v // blk) * blk + m - 1
            pieces.append(jnp.broadcast_to(b_ref[r:r + 1, :], (SUBLANES, HG_D)))
        else:
            acc = None
            for k in range(SUBLANES // blk):
                r = SUBLANES * v + k * blk + m - 1
                rowv = jnp.broadcast_to(b_ref[r:r + 1, :], (SUBLANES, HG_D))
                acc = rowv if acc is None else jnp.where(sub >= k * blk, rowv, acc)
            pieces.append(acc)
    return jnp.concatenate(pieces, axis=0)


def _hgrn_kernel(q_ref, f_ref, i_ref, g_ref, lbl_ref, nw_ref, o_ref, st_ref, b_scr):
    ci = pl.program_id(1)
    c_rows = HG_CHUNK

    @pl.when(ci == 0)
    def _reset():
        st_ref[...] = jnp.zeros_like(st_ref)

    lg = lbl_ref[...]
    e = jnp.exp(lg - jnp.max(lg, axis=0, keepdims=True))
    lb_all = e[0:1, :] / jnp.sum(e, axis=0, keepdims=True)

    row = lax.broadcasted_iota(I32, (c_rows, 1), 0)
    tt = lax.broadcasted_iota(I32, (c_rows, c_rows), 0)
    ss = lax.broadcasted_iota(I32, (c_rows, c_rows), 1)
    for c in range(HG_CHUNKS_PER_STEP):
        for hd in range(HG_HEADS):
            rs = slice(c * c_rows, (c + 1) * c_rows)
            cs = slice(hd * HG_D, (hd + 1) * HG_D)
            q = q_ref[rs, cs]
            v = i_ref[rs, cs]
            gg = g_ref[rs, cs]
            lb = lb_all[:, cs]
            f = lb + (1.0 - lb) * jax.nn.sigmoid(f_ref[rs, cs])
            k = 1.0 - f
            b = jnp.log(f)
            sh = 1
            while sh < c_rows:
                b = b + jnp.where(row >= sh, pltpu.roll(b, sh, 0), 0.0)
                sh *= 2
            b_scr[hd] = b
            bl = b[c_rows - 1:c_rows, :]
            st = st_ref[hd]
            vb = v.astype(BF16)
            o = _dot_nt((q * jnp.exp(b)).astype(BF16), st.astype(BF16))
            a = jnp.where(tt == ss, _dot_nt(q.astype(BF16), k.astype(BF16)), 0.0)
            m = 1
            while m < c_rows:
                ee = jnp.exp(-jnp.abs(b - _boundary_rows(b_scr.at[hd], m)))
                upper = (row & (2 * m - 1)) >= m
                qt = jnp.where(upper, q * ee, 0.0).astype(BF16)
                kt = jnp.where(upper, 0.0, k * ee).astype(BF16)
                shift = int(math.log2(2 * m))
                a = a + jnp.where((tt >> shift) == (ss >> shift), _dot_nt(qt, kt), 0.0)
                m *= 2
            o = o + _dot(a.astype(BF16), vb)
            st_ref[hd] = st * jnp.exp(bl) + _dot_tn(vb, (k * jnp.exp(bl - b)).astype(BF16))
            o = o * lax.rsqrt(jnp.mean(o * o, axis=-1, keepdims=True) + LN_EPS) * nw_ref[:, cs]
            o_ref[rs, cs] = (o * (gg * jax.nn.sigmoid(gg))).astype(o_ref.dtype)


def _hgrn(h, lb_logits, norm_w, *, batch, seq):
    t = h.shape[0]
    rows = HG_CHUNKS_PER_STEP * HG_CHUNK
    steps = seq // rows
    width = HG_HEADS * HG_D
    hb = COL_HG // width
    spec = lambda c: pl.BlockSpec((rows, width), lambda b, ci, c=c: (b * steps + ci, c))
    return pl.pallas_call(
        _hgrn_kernel,
        grid=(batch, steps),
        in_specs=[spec(hb), spec(hb + 1), spec(hb + 2), spec(hb + 3),
                  pl.BlockSpec(lb_logits.shape, lambda b, ci: (0, 0)),
                  pl.BlockSpec(norm_w.shape, lambda b, ci: (0, 0))],
        out_specs=pl.BlockSpec((rows, width), lambda b, ci: (b * steps + ci, 0)),
        out_shape=jax.ShapeDtypeStruct((t, width), BF16),
        scratch_shapes=[pltpu.VMEM((HG_HEADS, HG_D, HG_D), F32), pltpu.VMEM((HG_HEADS, HG_CHUNK, HG_D), F32)],
        compiler_params=_cparams(("arbitrary", "arbitrary")),
        name="hgrn",
    )(h, h, h, h, lb_logits, norm_w)


MID_TM = 256


def _mid_kernel(oa_ref, ob_ref, x_ref, kv_ref, wout_ref, wq_ref, wo_ref, pwq_ref, lng_ref, lnb_ref,
                x2_ref, x2b_ref, qry_ref):
    half = oa_ref.shape[1]
    d = x_ref.shape[1]
    mix = _dot(oa_ref[...], wout_ref[0:half, :]) + _dot(ob_ref[...], wout_ref[half:, :])
    x1 = _layer_norm(ALPHA * x_ref[...] + mix, lng_ref[0:1, :], lnb_ref[0:1, :])
    q = _dot(x1.astype(BF16), wq_ref[...])
    dh = d // XA_HEADS
    outs = []
    for hd in range(XA_HEADS):
        qh = q[:, hd * dh:(hd + 1) * dh].astype(BF16)
        kh = kv_ref[:, hd * dh:(hd + 1) * dh]
        vh = kv_ref[:, d + hd * dh:d + (hd + 1) * dh]
        s = _dot_nt(qh, kh) * dh ** -0.5
        s = s - jnp.max(s, axis=-1, keepdims=True)
        p = jnp.exp(s)
        p = p / jnp.sum(p, axis=-1, keepdims=True)
        outs.append(_dot(p.astype(BF16), vh))
    xa = _dot(jnp.concatenate(outs, axis=1).astype(BF16), wo_ref[...])
    x2 = _layer_norm(ALPHA * x1 + xa, lng_ref[1:2, :], lnb_ref[1:2, :])
    x2_ref[...] = x2
    x2b = x2.astype(BF16)
    x2b_ref[...] = x2b
    qry_ref[...] = _dot(x2b, pwq_ref[...]).astype(BF16)


def _mid(o_a, o_b, x, kvm, w_out, wq, wo, pwq, ln_g, ln_b, *, batch, seq):
    t, d = x.shape
    mem_len = kvm.shape[0] // batch
    steps = seq // MID_TM
    row = lambda b, i: (b * steps + i, 0)
    full = lambda b, i: (0, 0)
    half = o_a.shape[1]
    return pl.pallas_call(
        _mid_kernel,
        grid=(batch, steps),
        in_specs=[pl.BlockSpec((MID_TM, half), row), pl.BlockSpec((MID_TM, half), row), pl.BlockSpec((MID_TM, d), row),
                  pl.BlockSpec((mem_len, 2 * d), lambda b, i: (b, 0)),
                  pl.BlockSpec((d, d), full), pl.BlockSpec((d, d), full), pl.BlockSpec((d, d), full),
                  pl.BlockSpec((d, d), full), pl.BlockSpec(ln_g.shape, full), pl.BlockSpec(ln_b.shape, full)],
        out_specs=[pl.BlockSpec((MID_TM, d), row)] * 3,
        out_shape=[jax.ShapeDtypeStruct((t, d), F32), jax.ShapeDtypeStruct((t, d), BF16),
                   jax.ShapeDtypeStruct((t, d), BF16)],
        compiler_params=_cparams(("arbitrary", "arbitrary")),
        name="mid",
    )(o_a, o_b, x, kvm, w_out, wq, wo, pwq, ln_g, ln_b)


TOPK_TT = 128


def _top_rows(s, n):
    rows = s.shape[0]
    sub = lax.broadcasted_iota(I32, (rows, 1), 0)
    rn = lax.broadcasted_iota(I32, (n, 1), 0)
    vals = jnp.zeros((n, s.shape[1]), F32)
    idxs = jnp.zeros((n, s.shape[1]), I32)
    for a in range(n):
        mx = jnp.max(s, axis=0, keepdims=True)
        ix = jnp.min(jnp.where(s == mx, sub, rows), axis=0, keepdims=True)
        vals = jnp.where(rn == a, mx, vals)
        idxs = jnp.where(rn == a, ix, idxs)
        s = jnp.where(sub == ix, -jnp.inf, s)
    return vals, idxs


def _candidate_groups(s1, s2):
    k = PEER_TOPK
    sub = lax.broadcasted_iota(I32, (SUBLANES, 1), 0)
    groups = []
    for a0 in range(0, k, SUBLANES):
        groups.append((s1[a0:a0 + SUBLANES] + s2[0:1], (sub + a0) * k))
    for b in range(1, SUBLANES):
        n_valid = k // (b + 1)
        vals = jnp.where(sub < n_valid, s1[0:SUBLANES] + s2[b:b + 1], -jnp.inf)
        groups.append((vals, sub * k + b))
    for b0 in range(SUBLANES, k, SUBLANES):
        groups.append((s1[0:1] + s2[b0:b0 + SUBLANES], sub + b0))
    return groups


def _peer_topk_kernel(q_ref, key_ref, i1_ref, i2_ref, gate_ref):
    k = PEER_TOPK
    dk = key_ref.shape[2]
    q = q_ref[...]
    rk = lax.broadcasted_iota(I32, (k, 1), 0)
    big = k * k
    log_k = int(math.log2(k))
    assert 1 << log_k == k == 2 * SUBLANES
    i1_all, i2_all, g_all = [], [], []
    for hd in range(PEER_HEADS):
        tops = []
        for c in range(2):
            col = (2 * hd + c) * dk
            tops.append(_top_rows(_dot_nt(key_ref[2 * hd + c], q[:, col:col + dk]), k))
        (s1, i1), (s2, i2) = tops
        groups = _candidate_groups(s1, s2)
        top_s = jnp.zeros((k, q.shape[0]), F32)
        e1 = jnp.zeros((k, q.shape[0]), I32)
        e2 = jnp.zeros((k, q.shape[0]), I32)
        for j in range(k):
            mx = groups[0][0]
            for vals, _ in groups[1:]:
                mx = jnp.maximum(mx, vals)
            mx = jnp.max(mx, axis=0, keepdims=True)
            pm = None
            for vals, posn in groups:
                cand = jnp.where(vals == mx, posn, big)
                pm = cand if pm is None else jnp.minimum(pm, cand)
            pm = jnp.min(pm, axis=0, keepdims=True)
            a_sel = pm >> log_k
            b_sel = pm & (k - 1)
            i1_sel = jnp.sum(jnp.where(rk == a_sel, i1, 0), axis=0, keepdims=True)
            i2_sel = jnp.sum(jnp.where(rk == b_sel, i2, 0), axis=0, keepdims=True)
            top_s = jnp.where(rk == j, mx, top_s)
            e1 = jnp.where(rk == j, i1_sel, e1)
            e2 = jnp.where(rk == j, i2_sel, e2)
            groups = [(jnp.where(posn == pm, -jnp.inf, vals), posn) for vals, posn in groups]
        ex = jnp.exp(top_s - top_s[0:1])
        g_all.append(ex / jnp.sum(ex, axis=0, keepdims=True))
        i1_all.append(e1)
        i2_all.append(e2)
    i1_ref[...] = jnp.concatenate(i1_all, axis=0).T
    i2_ref[...] = jnp.concatenate(i2_all, axis=0).T
    gate_ref[...] = jnp.concatenate(g_all, axis=0).T


def _peer_topk(qry, keys):
    t, d = qry.shape
    n_sel = PEER_HEADS * PEER_TOPK
    row = lambda i: (i, 0)
    return pl.pallas_call(
        _peer_topk_kernel,
        grid=(t // TOPK_TT,),
        in_specs=[pl.BlockSpec((TOPK_TT, d), row), pl.BlockSpec(keys.shape, lambda i: (0, 0, 0))],
        out_specs=[pl.BlockSpec((TOPK_TT, n_sel), row)] * 3,
        out_shape=[jax.ShapeDtypeStruct((t, n_sel), I32), jax.ShapeDtypeStruct((t, n_sel), I32),
                   jax.ShapeDtypeStruct((t, n_sel), F32)],
        compiler_params=_cparams(("arbitrary",)),
        name="peer_topk",
    )(qry, keys)


def _peer_gbuild_kernel(i1_ref, i2_ref, gate_ref, o_ref):
    nk = PEER_NKEYS
    n_sel = i1_ref.shape[1]
    sub = lax.broadcasted_iota(I32, (nk, n_sel), 0)
    for r in range(GM_PITCH - GM_TILE):
        o_ref[pl.ds(GM_TILE + r, nk, stride=GM_PITCH), :] = jnp.zeros((nk, nk), F32)

    def body(t, carry):
        at = jnp.where(sub == i1_ref[pl.ds(t, 1), :], 1.0, 0.0).astype(BF16)
        bt = jnp.where(sub == i2_ref[pl.ds(t, 1), :], gate_ref[pl.ds(t, 1), :], 0.0).astype(BF16)
        o_ref[pl.ds(t, nk, stride=GM_PITCH), :] = _dot_nt(at, bt)
        return carry

    lax.fori_loop(0, GM_TILE, body, 0)


def _peer_gbuild(i1, i2, gate):
    t, n_sel = i1.shape
    row = lambda i: (i, 0)
    rows_out = PEER_NKEYS * GM_PITCH
    return pl.pallas_call(
        _peer_gbuild_kernel,
        grid=(t // GM_TILE,),
        in_specs=[pl.BlockSpec((GM_TILE, n_sel), row)] * 3,
        out_specs=pl.BlockSpec((rows_out, PEER_NKEYS), row),
        out_shape=jax.ShapeDtypeStruct((t // GM_TILE * rows_out, PEER_NKEYS), F32),
        compiler_params=_cparams(("arbitrary",)),
        name="peer_gbuild",
    )(i1, i2, gate)


DENSE_TT = 512
DENSE_NI = 8


def _peer_dense_kernel(xb_ref, x_ref, dn_ref, up_ref, gm_ref, lng_ref, lnb_ref, o_ref, acc_ref):
    j = pl.program_id(1)

    @pl.when(j == 0)
    def _zero():
        acc_ref[...] = jnp.zeros_like(acc_ref)

    hid = _dot_nt(xb_ref[...], dn_ref[...])
    act = 0.5 * hid * (1.0 + lax.erf(hid * (0.5 ** 0.5)))
    gm = jnp.concatenate(
        [jnp.concatenate([gm_ref[qq, ii, 0:GM_TILE, :] for qq in range(DENSE_TT // GM_TILE)], axis=0)
         for ii in range(DENSE_NI)], axis=1)
    acc_ref[...] += _dot((act * gm).astype(BF16), up_ref[...])

    @pl.when(j == pl.num_programs(1) - 1)
    def _finish():
        o_ref[...] = _layer_norm(ALPHA * x_ref[...] + acc_ref[...], lng_ref[2:3, :], lnb_ref[2:3, :])


def _peer_dense(xb, x, down, up, gmap, ln_g, ln_b):
    t, d = x.shape
    n_exp = down.shape[0]
    blk = DENSE_NI * PEER_NKEYS
    row = lambda i, j: (i, 0)
    full = lambda i, j: (0, 0)
    return pl.pallas_call(
        _peer_dense_kernel,
        grid=(t // DENSE_TT, n_exp // blk),
        in_specs=[pl.BlockSpec((DENSE_TT, d), row), pl.BlockSpec((DENSE_TT, d), row),
                  pl.BlockSpec((blk, d), lambda i, j: (j, 0)), pl.BlockSpec((blk, d), lambda i, j: (j, 0)),
                  pl.BlockSpec((DENSE_TT // GM_TILE, DENSE_NI, GM_PITCH, PEER_NKEYS), lambda i, j: (i, j, 0, 0)),
                  pl.BlockSpec(ln_g.shape, full), pl.BlockSpec(ln_b.shape, full)],
        out_specs=pl.BlockSpec((DENSE_TT, d), row),
        out_shape=jax.ShapeDtypeStruct((t, d), F32),
        scratch_shapes=[pltpu.VMEM((DENSE_TT, d), F32)],
        compiler_params=_cparams(("arbitrary", "arbitrary")),
        name="peer_dense",
    )(xb, x, down, up, gmap, ln_g, ln_b)


def _pack_w_in(w):
    nsa_w = KV_GROUPS * Q_PER_KV * HEAD_DIM
    kv_w = 6 * KV_GROUPS * HEAD_DIM
    n_gate = Q_PER_KV * 3
    gates = w[:, nsa_w + kv_w:nsa_w + kv_w + KV_GROUPS * n_gate]
    pad = jnp.zeros((w.shape[0], LANES - n_gate), w.dtype)
    parts = [w[:, :nsa_w + kv_w]]
    for g in range(KV_GROUPS):
        parts += [gates[:, g * n_gate:(g + 1) * n_gate], pad]
    parts.append(w[:, nsa_w + kv_w + KV_GROUPS * n_gate:])
    return jnp.concatenate(parts, axis=1).astype(BF16)


def _pad_cmp_w1(w1):
    w1r = w1.reshape(2, CMP_BLOCK, HEAD_DIM, w1.shape[-1]).astype(BF16)
    z = jnp.zeros_like(w1r)
    return jnp.stack([jnp.concatenate([w1r, z], axis=2), jnp.concatenate([z, w1r], axis=2)], axis=1)


def kernel(x, mem, w_in, cmp_pos, cmp_w1, cmp_w2, hgrn_lb_logits, hgrn_norm_w, w_out, xa_wq, xa_wkv, xa_wo, peer_wq, peer_subkeys, peer_down, peer_up, ln_g, ln_b):
    batch, seq, d = x.shape
    t = batch * seq
    assert w_in.shape[0] == DEPTH and w_in.shape[2] + 2 * LANES - KV_GROUPS * Q_PER_KV * 3 == PACKED_COLS
    x2d = x.reshape(t, d)

    h = _matmul(x2d, _pack_w_in(w_in[0]), F32, 256)
    pos2 = jnp.concatenate([cmp_pos[0]] * KV_GROUPS, axis=-1)
    o_a = _nsa(h, _pad_cmp_w1(cmp_w1[0]), pos2, cmp_w2[0].astype(BF16), batch=batch, seq=seq)
    o_b = _hgrn(h, hgrn_lb_logits.reshape(hgrn_lb_logits.shape[0], HG_HEADS * HG_D),
                hgrn_norm_w[0].reshape(1, HG_HEADS * HG_D), batch=batch, seq=seq)

    kvm = _matmul(mem.reshape(batch * mem.shape[1], d), xa_wkv[0].astype(BF16), BF16, 256)
    x2, x2b, qry = _mid(o_a, o_b, x2d, kvm, w_out[0].astype(BF16), xa_wq[0].astype(BF16), xa_wo[0].astype(BF16),
                        peer_wq[0].astype(BF16), ln_g[0], ln_b[0], batch=batch, seq=seq)

    keys = peer_subkeys[0].reshape(PEER_HEADS * 2, PEER_NKEYS, peer_subkeys.shape[-1]).astype(BF16)
    i1, i2, gate = _peer_topk(qry, keys)
    gmap = _peer_gbuild(i1, i2, gate).reshape(t // GM_TILE, PEER_NKEYS, GM_PITCH, PEER_NKEYS)
    out = _peer_dense(x2b, x2, peer_down[0].astype(BF16), peer_up[0].astype(BF16), gmap, ln_g[0], ln_b[0])
    return out.reshape(batch, seq, d)
```

```python
import functools
import math

import jax
import jax.numpy as jnp
import numpy as np
from jax import lax
from jax.experimental import pallas as pl
from jax.experimental.pallas import tpu as pltpu

F32 = jnp.float32
BF16 = jnp.bfloat16
I32 = jnp.int32

HEAD_DIM = 64
KV_GROUPS = 2
Q_PER_KV = 4
CMP_BLOCK = 32
CMP_STRIDE = 16
SLC_BLOCK = 64
SLC_TOPN = 16
WIN = 512
FORCE_SCORE = 1e4
HG_HEADS = 4
HG_D = 128
HG_CHUNK = 64
XA_HEADS = 4
PEER_HEADS = 8
PEER_NKEYS = 128
PEER_TOPK = 16
LN_EPS = 1e-5
NEG_INF = -1e30
DEPTH = 1
ALPHA = (2.0 * DEPTH) ** 0.25

LANES = 128
SUBLANES = 8
VMEM_LIMIT = 56 * 1024 * 1024

COL_Q = 0
COL_KV = 512
COL_GATE = 1280
COL_HG = 1536
PACKED_COLS = 3584

GM_TILE = 128
GM_PITCH = 136
GM_UNROLL = 32


def _dot(a, b):
    return jnp.dot(a, b, preferred_element_type=F32)


def _dot_nt(a, b):
    return lax.dot_general(a, b, (((1,), (1,)), ((), ())), preferred_element_type=F32)


def _dot_tn(a, b):
    return lax.dot_general(a, b, (((0,), (0,)), ((), ())), preferred_element_type=F32)


def _layer_norm(y, g, b):
    mu = jnp.mean(y, axis=-1, keepdims=True)
    d = y - mu
    var = jnp.mean(d * d, axis=-1, keepdims=True)
    return d * lax.rsqrt(var + LN_EPS) * g + b


def _cparams(sem):
    return pltpu.CompilerParams(dimension_semantics=sem, vmem_limit_bytes=VMEM_LIMIT)


def _matmul_kernel(x_ref, w_ref, o_ref):
    o_ref[...] = _dot(x_ref[...].astype(BF16), w_ref[...]).astype(o_ref.dtype)


def _matmul(x, w, out_dtype, tm):
    m, k = x.shape
    n = w.shape[1]
    return pl.pallas_call(
        _matmul_kernel,
        grid=(m // tm,),
        in_specs=[pl.BlockSpec((tm, k), lambda i: (i, 0)), pl.BlockSpec((k, n), lambda i: (0, 0))],
        out_specs=pl.BlockSpec((tm, n), lambda i: (i, 0)),
        out_shape=jax.ShapeDtypeStruct((m, n), out_dtype),
        compiler_params=_cparams(("arbitrary",)),
        name="matmul",
    )(x, w)


NSA_TQ = 128
NSA_TK = 256
NSA_TKW = 128


def _online_softmax_step(qs, k_tile, v_tile, bias, carry):
    m, l, acc = carry
    s = _dot_nt(qs, k_tile) + bias
    m_new = jnp.maximum(m, jnp.max(s, axis=-1, keepdims=True))
    alpha = jnp.exp(m - m_new)
    p = jnp.exp(s - m_new)
    l = alpha * l + jnp.sum(p, axis=-1, keepdims=True)
    acc = alpha * acc + _dot(p.astype(BF16), v_tile)
    return m_new, l, acc


def _nsa_kernel(q_ref, kc_ref, vc_ref, ks_ref, vs_ref, kw_ref, vw_ref, gt_ref, w1_ref, pos_ref, w2_ref, ov_ref,
                e_ref, o_ref, kcmp_s, vcmp_s, ks_s, vs_s, kw_s, vw_s, *, seq, n_top):
    g = pl.program_id(1)
    qi = pl.program_id(2)
    tq, dh, r_heads = NSA_TQ, HEAD_DIM, Q_PER_KV
    rows = r_heads * tq
    scale = dh ** -0.5
    assert math.frexp(scale)[0] == 0.5
    n_cmp_pad = seq // CMP_STRIDE

    @pl.when(qi == 0)
    def _prepare():
        def pick(ref):
            xx = ref[...]
            return jnp.where(g == 0, xx[:, :dh], xx[:, dh:]).astype(BF16)

        ks_s[...] = pick(ks_ref)
        vs_s[...] = pick(vs_ref)
        kw_s[...] = pick(kw_ref)
        vw_s[...] = pick(vw_ref)
        half = CMP_BLOCK // 2
        for kv, (src, dst) in enumerate(((kc_ref, kcmp_s), (vc_ref, vcmp_s))):
            ya = jnp.zeros((n_cmp_pad, CMP_BLOCK * 4), F32)
            yb = jnp.zeros((n_cmp_pad, CMP_BLOCK * 4), F32)
            for l in range(half):
                xl = src[pl.ds(l, n_cmp_pad, stride=CMP_STRIDE), :]
                ya = ya + _dot((xl + pos_ref[kv, l:l + 1, :]).astype(BF16), w1_ref[kv, l])
                yb = yb + _dot((xl + pos_ref[kv, half + l:half + l + 1, :]).astype(BF16), w1_ref[kv, half + l])
            h1 = ya + pltpu.roll(yb, n_cmp_pad - 1, 0)
            h1 = jax.nn.gelu(h1, approximate=True)
            dst[...] = _dot(h1.astype(BF16), w2_ref[kv]).astype(BF16)

    q0 = qi * tq
    q4 = q_ref[...] * scale
    qs = jnp.concatenate([q4[:, r * dh:(r + 1) * dh] for r in range(r_heads)], axis=0).astype(BF16)
    pos = q0 + (lax.broadcasted_iota(I32, (rows, 1), 0) & (tq - 1))
    tpos = q0 + lax.broadcasted_iota(I32, (tq, 1), 0)

    s = _dot_nt(qs, kcmp_s[...])
    n_idx = lax.broadcasted_iota(I32, (1, n_cmp_pad), 1)
    valid = (n_idx * CMP_STRIDE + (CMP_BLOCK - 1)) <= pos
    s = jnp.where(valid, s, NEG_INF)
    s = s - jnp.max(s, axis=-1, keepdims=True)
    p = jnp.exp(s)
    p = p / jnp.sum(p, axis=-1, keepdims=True)
    p = jnp.where(pos >= CMP_BLOCK - 1, p, 0.0).astype(BF16)
    o_cmp = _dot(p, vcmp_s[...])
    imp4 = _dot(p, ov_ref[...])
    imp = imp4[0:tq]
    for r in range(1, r_heads):
        imp = imp + imp4[r * tq:(r + 1) * tq]

    n_slc = seq // SLC_BLOCK
    sc = imp.T[0:n_slc, :]
    j_idx = lax.broadcasted_iota(I32, (n_slc, 1), 0)
    cur = (q0 + lax.broadcasted_iota(I32, (1, tq), 1)) // SLC_BLOCK
    forced = (j_idx == 0) | (j_idx == cur) | (j_idx == cur - 1)
    score = jnp.where(j_idx > cur, -jnp.inf, jnp.where(forced, FORCE_SCORE, sc))
    cnt = jnp.zeros((n_slc, tq), F32)
    for i in range(n_slc):
        ri = score[i:i + 1, :]
        beats = (ri > score) | ((ri == score) & (j_idx > i))
        cnt = cnt + jnp.where(beats, 1.0, 0.0)
    sel_t = jnp.where(cnt < n_top, 1.0, 0.0)
    sel_t = jnp.concatenate([sel_t, jnp.zeros((LANES - n_slc, tq), F32)], axis=0)
    sel = sel_t.T.astype(BF16)

    def slc_body(kt, carry):
        k0 = pl.multiple_of(kt * NSA_TK, NSA_TK)
        mf = _dot(sel, e_ref[kt])
        kpos = k0 + lax.broadcasted_iota(I32, (1, NSA_TK), 1)
        bias = jnp.where((mf > 0.5) & (kpos <= tpos), 0.0, NEG_INF)
        bias = jnp.concatenate([bias] * r_heads, axis=0)
        return _online_softmax_step(qs, ks_s[pl.ds(k0, NSA_TK), :], vs_s[pl.ds(k0, NSA_TK), :], bias, carry)

    init = (jnp.full((rows, 1), NEG_INF, F32), jnp.zeros((rows, 1), F32), jnp.zeros((rows, dh), F32))
    n_kt = (q0 + tq + NSA_TK - 1) // NSA_TK
    _, l_s, acc_s = lax.fori_loop(0, n_kt, slc_body, init)
    o_slc = acc_s / l_s

    def win_body(i, carry):
        k0 = pl.multiple_of(q0 - WIN + i * NSA_TKW, NSA_TKW)
        kpos = k0 + lax.broadcasted_iota(I32, (1, NSA_TKW), 1)
        bias = jnp.where((kpos <= tpos) & (kpos > tpos - WIN), 0.0, NEG_INF)
        bias = jnp.concatenate([bias] * r_heads, axis=0)
        return _online_softmax_step(qs, kw_s[pl.ds(k0, NSA_TKW), :], vw_s[pl.ds(k0, NSA_TKW), :], bias, carry)

    n_wt = (WIN + tq) // NSA_TKW
    first = jnp.maximum(0, (WIN - q0) // NSA_TKW)
    _, l_w, acc_w = lax.fori_loop(first, n_wt, win_body, init)
    o_win = acc_w / l_w

    gate = jax.nn.sigmoid(gt_ref[...])
    outs = []
    for r in range(r_heads):
        sl = slice(r * tq, (r + 1) * tq)
        outs.append(gate[:, 3 * r:3 * r + 1] * o_cmp[sl] + gate[:, 3 * r + 1:3 * r + 2] * o_slc[sl]
                    + gate[:, 3 * r + 2:3 * r + 3] * o_win[sl])
    o_ref[...] = jnp.concatenate(outs, axis=1).astype(o_ref.dtype)


def _nsa_constants(seq):
    n_cmp = (seq - CMP_BLOCK) // CMP_STRIDE + 1
    n_slc = seq // SLC_BLOCK
    cmp_start = np.arange(n_cmp) * CMP_STRIDE
    slc_start = np.arange(n_slc) * SLC_BLOCK
    overlap = ((cmp_start[:, None] < slc_start[None, :] + SLC_BLOCK)
               & (cmp_start[:, None] + CMP_BLOCK > slc_start[None, :])).astype(np.float32)
    ov = np.zeros((seq // CMP_STRIDE, LANES), np.float32)
    ov[:n_cmp, :n_slc] = overlap
    key = np.arange(seq)
    expand = (key[None, :] // SLC_BLOCK == np.arange(LANES)[:, None]).astype(np.float32)
    expand = expand.reshape(LANES, seq // NSA_TK, NSA_TK).transpose(1, 0, 2)
    return jnp.asarray(ov, BF16), jnp.asarray(expand, BF16)


def _nsa(h, w1pad, pos2, w2, *, batch, seq):
    t = h.shape[0]
    nq = seq // NSA_TQ
    ov, expand = _nsa_constants(seq)
    n_top = min(SLC_TOPN, seq // SLC_BLOCK)
    kv_spec = lambda c: pl.BlockSpec((seq, LANES), lambda b, g, qi, c=c: (b, c))
    kvb = COL_KV // LANES
    in_specs = [
        pl.BlockSpec((NSA_TQ, Q_PER_KV * HEAD_DIM), lambda b, g, qi: (b * nq + qi, g)),
        kv_spec(kvb), kv_spec(kvb + 1), kv_spec(kvb + 2), kv_spec(kvb + 3), kv_spec(kvb + 4), kv_spec(kvb + 5),
        pl.BlockSpec((NSA_TQ, LANES), lambda b, g, qi: (b * nq + qi, COL_GATE // LANES + g)),
        pl.BlockSpec((2, None, CMP_BLOCK, LANES, LANES), lambda b, g, qi: (0, g, 0, 0, 0)),
        pl.BlockSpec(pos2.shape, lambda b, g, qi: (0, 0, 0)),
        pl.BlockSpec(w2.shape, lambda b, g, qi: (0, 0, 0)),
        pl.BlockSpec(ov.shape, lambda b, g, qi: (0, 0)),
        pl.BlockSpec(expand.shape, lambda b, g, qi: (0, 0, 0)),
    ]
    n_cmp_pad = seq // CMP_STRIDE
    scratch = [pltpu.VMEM((n_cmp_pad, HEAD_DIM), BF16), pltpu.VMEM((n_cmp_pad, HEAD_DIM), BF16)] + [
        pltpu.VMEM((seq, HEAD_DIM), BF16) for _ in range(4)]
    return pl.pallas_call(
        functools.partial(_nsa_kernel, seq=seq, n_top=n_top),
        grid=(batch, KV_GROUPS, nq),
        in_specs=in_specs,
        out_specs=pl.BlockSpec((NSA_TQ, Q_PER_KV * HEAD_DIM), lambda b, g, qi: (b * nq + qi, g)),
        out_shape=jax.ShapeDtypeStruct((t, KV_GROUPS * Q_PER_KV * HEAD_DIM), BF16),
        scratch_shapes=scratch,
        compiler_params=_cparams(("arbitrary", "arbitrary", "arbitrary")),
        name="nsa",
    )(h, h, h, h, h, h, h, h, w1pad, pos2, w2, ov, expand)


HG_CHUNKS_PER_STEP = 2


def _boundary_rows(b_ref, m):
    blk = 2 * m
    pieces = []
    sub = lax.broadcasted_iota(I32, (SUBLANES, 1), 0)
    for v in range(HG_CHUNK // SUBLANES):
        if blk >= SUBLANES:
            r = (SUBLANES * v // blk) * blk + m - 1
            pieces.append(jnp.broadcast_to(b_ref[r:r + 1, :], (SUBLANES, HG_D)))
        else:
            acc = None
            for k in range(SUBLANES // blk):
                r = SUBLANES * v + k * blk + m - 1
                rowv = jnp.broadcast_to(b_ref[r:r + 1, :], (SUBLANES, HG_D))
                acc = rowv if acc is None else jnp.where(sub >= k * blk, rowv, acc)
            pieces.append(acc)
    return jnp.concatenate(pieces, axis=0)


def _hgrn_kernel(q_ref, f_ref, i_ref, g_ref, lbl_ref, nw_ref, o_ref, st_ref, b_scr):
    ci = pl.program_id(1)
    c_rows = HG_CHUNK

    @pl.when(ci == 0)
    def _reset():
        st_ref[...] = jnp.zeros_like(st_ref)

    lg = lbl_ref[...]
    e = jnp.exp(lg - jnp.max(lg, axis=0, keepdims=True))
    lb_all = e[0:1, :] / jnp.sum(e, axis=0, keepdims=True)

    row = lax.broadcasted_iota(I32, (c_rows, 1), 0)
    tt = lax.broadcasted_iota(I32, (c_rows, c_rows), 0)
    ss = lax.broadcasted_iota(I32, (c_rows, c_rows), 1)
    for c in range(HG_CHUNKS_PER_STEP):
        for hd in range(HG_HEADS):
            rs = slice(c * c_rows, (c + 1) * c_rows)
            cs = slice(hd * HG_D, (hd + 1) * HG_D)
            q = q_ref[rs, cs]
            v = i_ref[rs, cs]
            gg = g_ref[rs, cs]
            lb = lb_all[:, cs]
            f = lb + (1.0 - lb) * jax.nn.sigmoid(f_ref[rs, cs])
            k = 1.0 - f
            b = jnp.log(f)
            sh = 1
            while sh < c_rows:
                b = b + jnp.where(row >= sh, pltpu.roll(b, sh, 0), 0.0)
                sh *= 2
            b_scr[hd] = b
            bl = b[c_rows - 1:c_rows, :]
            st = st_ref[hd]
            vb = v.astype(BF16)
            o = _dot_nt((q * jnp.exp(b)).astype(BF16), st.astype(BF16))
            a = jnp.where(tt == ss, _dot_nt(q.astype(BF16), k.astype(BF16)), 0.0)
            m = 1
            while m < c_rows:
                ee = jnp.exp(-jnp.abs(b - _boundary_rows(b_scr.at[hd], m)))
                upper = (row & (2 * m - 1)) >= m
                qt = jnp.where(upper, q * ee, 0.0).astype(BF16)
                kt = jnp.where(upper, 0.0, k * ee).astype(BF16)
                shift = int(math.log2(2 * m))
                a = a + jnp.where((tt >> shift) == (ss >> shift), _dot_nt(qt, kt), 0.0)
                m *= 2
            o = o + _dot(a.astype(BF16), vb)
            st_ref[hd] = st * jnp.exp(bl) + _dot_tn(vb, (k * jnp.exp(bl - b)).astype(BF16))
            o = o * lax.rsqrt(jnp.mean(o * o, axis=-1, keepdims=True) + LN_EPS) * nw_ref[:, cs]
            o_ref[rs, cs] = (o * (gg * jax.nn.sigmoid(gg))).astype(o_ref.dtype)


def _hgrn(h, lb_logits, norm_w, *, batch, seq):
    t = h.shape[0]
    rows = HG_CHUNKS_PER_STEP * HG_CHUNK
    steps = seq // rows
    width = HG_HEADS * HG_D
    hb = COL_HG // width
    spec = lambda c: pl.BlockSpec((rows, width), lambda b, ci, c=c: (b * steps + ci, c))
    return pl.pallas_call(
        _hgrn_kernel,
        grid=(batch, steps),
        in_specs=[spec(hb), spec(hb + 1), spec(hb + 2), spec(hb + 3),
                  pl.BlockSpec(lb_logits.shape, lambda b, ci: (0, 0)),
                  pl.BlockSpec(norm_w.shape, lambda b, ci: (0, 0))],
        out_specs=pl.BlockSpec((rows, width), lambda b, ci: (b * steps + ci, 0)),
        out_shape=jax.ShapeDtypeStruct((t, width), BF16),
        scratch_shapes=[pltpu.VMEM((HG_HEADS, HG_D, HG_D), F32), pltpu.VMEM((HG_HEADS, HG_CHUNK, HG_D), F32)],
        compiler_params=_cparams(("arbitrary", "arbitrary")),
        name="hgrn",
    )(h, h, h, h, lb_logits, norm_w)


MID_TM = 256


def _mid_kernel(oa_ref, ob_ref, x_ref, kv_ref, wout_ref, wq_ref, wo_ref, pwq_ref, lng_ref, lnb_ref,
                x2_ref, x2b_ref, qry_ref):
    half = oa_ref.shape[1]
    d = x_ref.shape[1]
    mix = _dot(oa_ref[...], wout_ref[0:half, :]) + _dot(ob_ref[...], wout_ref[half:, :])
    x1 = _layer_norm(ALPHA * x_ref[...] + mix, lng_ref[0:1, :], lnb_ref[0:1, :])
    q = _dot(x1.astype(BF16), wq_ref[...])
    dh = d // XA_HEADS
    outs = []
    for hd in range(XA_HEADS):
        qh = q[:, hd * dh:(hd + 1) * dh].astype(BF16)
        kh = kv_ref[:, hd * dh:(hd + 1) * dh]
        vh = kv_ref[:, d + hd * dh:d + (hd + 1) * dh]
        s = _dot_nt(qh, kh) * dh ** -0.5
        s = s - jnp.max(s, axis=-1, keepdims=True)
        p = jnp.exp(s)
        p = p / jnp.sum(p, axis=-1, keepdims=True)
        outs.append(_dot(p.astype(BF16), vh))
    xa = _dot(jnp.concatenate(outs, axis=1).astype(BF16), wo_ref[...])
    x2 = _layer_norm(ALPHA * x1 + xa, lng_ref[1:2, :], lnb_ref[1:2, :])
    x2_ref[...] = x2
    x2b = x2.astype(BF16)
    x2b_ref[...] = x2b
    qry_ref[...] = _dot(x2b, pwq_ref[...]).astype(BF16)


def _mid(o_a, o_b, x, kvm, w_out, wq, wo, pwq, ln_g, ln_b, *, batch, seq):
    t, d = x.shape
    mem_len = kvm.shape[0] // batch
    steps = seq // MID_TM
    row = lambda b, i: (b * steps + i, 0)
    full = lambda b, i: (0, 0)
    half = o_a.shape[1]
    return pl.pallas_call(
        _mid_kernel,
        grid=(batch, steps),
        in_specs=[pl.BlockSpec((MID_TM, half), row), pl.BlockSpec((MID_TM, half), row), pl.BlockSpec((MID_TM, d), row),
                  pl.BlockSpec((mem_len, 2 * d), lambda b, i: (b, 0)),
                  pl.BlockSpec((d, d), full), pl.BlockSpec((d, d), full), pl.BlockSpec((d, d), full),
                  pl.BlockSpec((d, d), full), pl.BlockSpec(ln_g.shape, full), pl.BlockSpec(ln_b.shape, full)],
        out_specs=[pl.BlockSpec((MID_TM, d), row)] * 3,
        out_shape=[jax.ShapeDtypeStruct((t, d), F32), jax.ShapeDtypeStruct((t, d), BF16),
                   jax.ShapeDtypeStruct((t, d), BF16)],
        compiler_params=_cparams(("arbitrary", "arbitrary")),
        name="mid",
    )(o_a, o_b, x, kvm, w_out, wq, wo, pwq, ln_g, ln_b)


TOPK_TT = 128


def _top_rows(s, n):
    rows = s.shape[0]
    sub = lax.broadcasted_iota(I32, (rows, 1), 0)
    rn = lax.broadcasted_iota(I32, (n, 1), 0)
    vals = jnp.zeros((n, s.shape[1]), F32)
    idxs = jnp.zeros((n, s.shape[1]), I32)
    for a in range(n):
        mx = jnp.max(s, axis=0, keepdims=True)
        ix = jnp.min(jnp.where(s == mx, sub, rows), axis=0, keepdims=True)
        vals = jnp.where(rn == a, mx, vals)
        idxs = jnp.where(rn == a, ix, idxs)
        s = jnp.where(sub == ix, -jnp.inf, s)
    return vals, idxs


def _candidate_groups(s1, s2):
    k = PEER_TOPK
    sub = lax.broadcasted_iota(I32, (SUBLANES, 1), 0)
    groups = []
    for a0 in range(0, k, SUBLANES):
        groups.append((s1[a0:a0 + SUBLANES] + s2[0:1], (sub + a0) * k))
    for b in range(1, SUBLANES):
        n_valid = k // (b + 1)
        vals = jnp.where(sub < n_valid, s1[0:SUBLANES] + s2[b:b + 1], -jnp.inf)
        groups.append((vals, sub * k + b))
    for b0 in range(SUBLANES, k, SUBLANES):
        groups.append((s1[0:1] + s2[b0:b0 + SUBLANES], sub + b0))
    return groups


def _peer_topk_kernel(q_ref, key_ref, i1_ref, i2_ref, gate_ref):
    k = PEER_TOPK
    dk = key_ref.shape[2]
    q = q_ref[...]
    rk = lax.broadcasted_iota(I32, (k, 1), 0)
    big = k * k
    log_k = int(math.log2(k))
    assert 1 << log_k == k == 2 * SUBLANES
    i1_all, i2_all, g_all = [], [], []
    for hd in range(PEER_HEADS):
        tops = []
        for c in range(2):
            col = (2 * hd + c) * dk
            tops.append(_top_rows(_dot_nt(key_ref[2 * hd + c], q[:, col:col + dk]), k))
        (s1, i1), (s2, i2) = tops
        groups = _candidate_groups(s1, s2)
        top_s = jnp.zeros((k, q.shape[0]), F32)
        e1 = jnp.zeros((k, q.shape[0]), I32)
        e2 = jnp.zeros((k, q.shape[0]), I32)
        for j in range(k):
            mx = groups[0][0]
            for vals, _ in groups[1:]:
                mx = jnp.maximum(mx, vals)
            mx = jnp.max(mx, axis=0, keepdims=True)
            pm = None
            for vals, posn in groups:
                cand = jnp.where(vals == mx, posn, big)
                pm = cand if pm is None else jnp.minimum(pm, cand)
            pm = jnp.min(pm, axis=0, keepdims=True)
            a_sel = pm >> log_k
            b_sel = pm & (k - 1)
            i1_sel = jnp.sum(jnp.where(rk == a_sel, i1, 0), axis=0, keepdims=True)
            i2_sel = jnp.sum(jnp.where(rk == b_sel, i2, 0), axis=0, keepdims=True)
            top_s = jnp.where(rk == j, mx, top_s)
            e1 = jnp.where(rk == j, i1_sel, e1)
            e2 = jnp.where(rk == j, i2_sel, e2)
            groups = [(jnp.where(posn == pm, -jnp.inf, vals), posn) for vals, posn in groups]
        ex = jnp.exp(top_s - top_s[0:1])
        g_all.append(ex / jnp.sum(ex, axis=0, keepdims=True))
        i1_all.append(e1)
        i2_all.append(e2)
    i1_ref[...] = jnp.concatenate(i1_all, axis=0).T
    i2_ref[...] = jnp.concatenate(i2_all, axis=0).T
    gate_ref[...] = jnp.concatenate(g_all, axis=0).T


def _peer_topk(qry, keys):
    t, d = qry.shape
    n_sel = PEER_HEADS * PEER_TOPK
    row = lambda i: (i, 0)
    return pl.pallas_call(
        _peer_topk_kernel,
        grid=(t // TOPK_TT,),
        in_specs=[pl.BlockSpec((TOPK_TT, d), row), pl.BlockSpec(keys.shape, lambda i: (0, 0, 0))],
        out_specs=[pl.BlockSpec((TOPK_TT, n_sel), row)] * 3,
        out_shape=[jax.ShapeDtypeStruct((t, n_sel), I32), jax.ShapeDtypeStruct((t, n_sel), I32),
                   jax.ShapeDtypeStruct((t, n_sel), F32)],
        compiler_params=_cparams(("arbitrary",)),
        name="peer_topk",
    )(qry, keys)


def _peer_gbuild_kernel(i1_ref, i2_ref, gate_ref, o_ref):
    nk = PEER_NKEYS
    n_sel = i1_ref.shape[1]
    sub = lax.broadcasted_iota(I32, (nk, n_sel), 0)
    for r in range(GM_PITCH - GM_TILE):
        o_ref[pl.ds(GM_TILE + r, nk, stride=GM_PITCH), :] = jnp.zeros((nk, nk), F32)

    def body(t, carry):
        at = jnp.where(sub == i1_ref[pl.ds(t, 1), :], 1.0, 0.0).astype(BF16)
        bt = jnp.where(sub == i2_ref[pl.ds(t, 1), :], gate_ref[pl.ds(t, 1), :], 0.0).astype(BF16)
        o_ref[pl.ds(t, nk, stride=GM_PITCH), :] = _dot_nt(at, bt)
        return carry

    lax.fori_loop(0, GM_TILE, body, 0, unroll=GM_UNROLL)


def _peer_gbuild(i1, i2, gate):
    t, n_sel = i1.shape
    row = lambda i: (i, 0)
    rows_out = PEER_NKEYS * GM_PITCH
    return pl.pallas_call(
        _peer_gbuild_kernel,
        grid=(t // GM_TILE,),
        in_specs=[pl.BlockSpec((GM_TILE, n_sel), row)] * 3,
        out_specs=pl.BlockSpec((rows_out, PEER_NKEYS), row),
        out_shape=jax.ShapeDtypeStruct((t // GM_TILE * rows_out, PEER_NKEYS), F32),
        compiler_params=_cparams(("arbitrary",)),
        name="peer_gbuild",
    )(i1, i2, gate)


DENSE_TT = 512
DENSE_NI = 8


def _peer_dense_kernel(xb_ref, x_ref, dn_ref, up_ref, gm_ref, lng_ref, lnb_ref, o_ref, acc_ref):
    j = pl.program_id(1)

    @pl.when(j == 0)
    def _zero():
        acc_ref[...] = jnp.zeros_like(acc_ref)

    hid = _dot_nt(xb_ref[...], dn_ref[...])
    act = 0.5 * hid * (1.0 + lax.erf(hid * (0.5 ** 0.5)))
    gm = jnp.concatenate(
        [jnp.concatenate([gm_ref[qq, ii, 0:GM_TILE, :] for qq in range(DENSE_TT // GM_TILE)], axis=0)
         for ii in range(DENSE_NI)], axis=1)
    acc_ref[...] += _dot((act * gm).astype(BF16), up_ref[...])

    @pl.when(j == pl.num_programs(1) - 1)
    def _finish():
        o_ref[...] = _layer_norm(ALPHA * x_ref[...] + acc_ref[...], lng_ref[2:3, :], lnb_ref[2:3, :])


def _peer_dense(xb, x, down, up, gmap, ln_g, ln_b):
    t, d = x.shape
    n_exp = down.shape[0]
    blk = DENSE_NI * PEER_NKEYS
    row = lambda i, j: (i, 0)
    full = lambda i, j: (0, 0)
    return pl.pallas_call(
        _peer_dense_kernel,
        grid=(t // DENSE_TT, n_exp // blk),
        in_specs=[pl.BlockSpec((DENSE_TT, d), row), pl.BlockSpec((DENSE_TT, d), row),
                  pl.BlockSpec((blk, d), lambda i, j: (j, 0)), pl.BlockSpec((blk, d), lambda i, j: (j, 0)),
                  pl.BlockSpec((DENSE_TT // GM_TILE, DENSE_NI, GM_PITCH, PEER_NKEYS), lambda i, j: (i, j, 0, 0)),
                  pl.BlockSpec(ln_g.shape, full), pl.BlockSpec(ln_b.shape, full)],
        out_specs=pl.BlockSpec((DENSE_TT, d), row),
        out_shape=jax.ShapeDtypeStruct((t, d), F32),
        scratch_shapes=[pltpu.VMEM((DENSE_TT, d), F32)],
        compiler_params=_cparams(("arbitrary", "arbitrary")),
        name="peer_dense",
    )(xb, x, down, up, gmap, ln_g, ln_b)


def _pack_w_in(w):
    nsa_w = KV_GROUPS * Q_PER_KV * HEAD_DIM
    kv_w = 6 * KV_GROUPS * HEAD_DIM
    n_gate = Q_PER_KV * 3
    gates = w[:, nsa_w + kv_w:nsa_w + kv_w + KV_GROUPS * n_gate]
    pad = jnp.zeros((w.shape[0], LANES - n_gate), w.dtype)
    parts = [w[:, :nsa_w + kv_w]]
    for g in range(KV_GROUPS):
        parts += [gates[:, g * n_gate:(g + 1) * n_gate], pad]
    parts.append(w[:, nsa_w + kv_w + KV_GROUPS * n_gate:])
    return jnp.concatenate(parts, axis=1).astype(BF16)


def _pad_cmp_w1(w1):
    w1r = w1.reshape(2, CMP_BLOCK, HEAD_DIM, w1.shape[-1]).astype(BF16)
    z = jnp.zeros_like(w1r)
    return jnp.stack([jnp.concatenate([w1r, z], axis=2), jnp.concatenate([z, w1r], axis=2)], axis=1)


def kernel(x, mem, w_in, cmp_pos, cmp_w1, cmp_w2, hgrn_lb_logits, hgrn_norm_w, w_out, xa_wq, xa_wkv, xa_wo, peer_wq, peer_subkeys, peer_down, peer_up, ln_g, ln_b):
    batch, seq, d = x.shape
    t = batch * seq
    assert w_in.shape[0] == DEPTH and w_in.shape[2] + 2 * LANES - KV_GROUPS * Q_PER_KV * 3 == PACKED_COLS
    x2d = x.reshape(t, d)

    h = _matmul(x2d, _pack_w_in(w_in[0]), F32, 256)
    pos2 = jnp.concatenate([cmp_pos[0]] * KV_GROUPS, axis=-1)
    o_a = _nsa(h, _pad_cmp_w1(cmp_w1[0]), pos2, cmp_w2[0].astype(BF16), batch=batch, seq=seq)
    o_b = _hgrn(h, hgrn_lb_logits.reshape(hgrn_lb_logits.shape[0], HG_HEADS * HG_D),
                hgrn_norm_w[0].reshape(1, HG_HEADS * HG_D), batch=batch, seq=seq)

    kvm = _matmul(mem.reshape(batch * mem.shape[1], d), xa_wkv[0].astype(BF16), BF16, 256)
    x2, x2b, qry = _mid(o_a, o_b, x2d, kvm, w_out[0].astype(BF16), xa_wq[0].astype(BF16), xa_wo[0].astype(BF16),
                        peer_wq[0].astype(BF16), ln_g[0], ln_b[0], batch=batch, seq=seq)

    keys = peer_subkeys[0].reshape(PEER_HEADS * 2, PEER_NKEYS, peer_subkeys.shape[-1]).astype(BF16)
    i1, i2, gate = _peer_topk(qry, keys)
    gmap = _peer_gbuild(i1, i2, gate).reshape(t // GM_TILE, PEER_NKEYS, GM_PITCH, PEER_NKEYS)
    out = _peer_dense(x2b, x2, peer_down[0].astype(BF16), peer_up[0].astype(BF16), gmap, ln_g[0], ln_b[0])
    return out.reshape(batch, seq, d)
```

```python
import functools
import math

import jax
import jax.numpy as jnp
import numpy as np
from jax import lax
from jax.experimental import pallas as pl
from jax.experimental.pallas import tpu as pltpu

F32 = jnp.float32
BF16 = jnp.bfloat16
I32 = jnp.int32

HEAD_DIM = 64
KV_GROUPS = 2
Q_PER_KV = 4
CMP_BLOCK = 32
CMP_STRIDE = 16
SLC_BLOCK = 64
SLC_TOPN = 16
WIN = 512
FORCE_SCORE = 1e4
HG_HEADS = 4
HG_D = 128
HG_CHUNK = 64
XA_HEADS = 4
PEER_HEADS = 8
PEER_NKEYS = 128
PEER_TOPK = 16
LN_EPS = 1e-5
NEG_INF = -1e30
DEPTH = 1
ALPHA = (2.0 * DEPTH) ** 0.25

LANES = 128
SUBLANES = 8
VMEM_LIMIT = 56 * 1024 * 1024

COL_Q = 0
COL_KV = 512
COL_GATE = 1280
COL_HG = 1536
PACKED_COLS = 3584

GM_TILE = 128
GM_PITCH = 136


def _dot(a, b):
    return jnp.dot(a, b, preferred_element_type=F32)


def _dot_nt(a, b):
    return lax.dot_general(a, b, (((1,), (1,)), ((), ())), preferred_element_type=F32)


def _dot_tn(a, b):
    return lax.dot_general(a, b, (((0,), (0,)), ((), ())), preferred_element_type=F32)


def _layer_norm(y, g, b):
    mu = jnp.mean(y, axis=-1, keepdims=True)
    d = y - mu
    var = jnp.mean(d * d, axis=-1, keepdims=True)
    return d * lax.rsqrt(var + LN_EPS) * g + b


def _cparams(sem):
    return pltpu.CompilerParams(dimension_semantics=sem, vmem_limit_bytes=VMEM_LIMIT)


def _matmul_kernel(x_ref, w_ref, o_ref):
    o_ref[...] = _dot(x_ref[...].astype(BF16), w_ref[...]).astype(o_ref.dtype)


def _matmul(x, w, out_dtype, tm):
    m, k = x.shape
    n = w.shape[1]
    return pl.pallas_call(
        _matmul_kernel,
        grid=(m // tm,),
        in_specs=[pl.BlockSpec((tm, k), lambda i: (i, 0)), pl.BlockSpec((k, n), lambda i: (0, 0))],
        out_specs=pl.BlockSpec((tm, n), lambda i: (i, 0)),
        out_shape=jax.ShapeDtypeStruct((m, n), out_dtype),
        compiler_params=_cparams(("arbitrary",)),
        name="matmul",
    )(x, w)


NSA_TQ = 128
NSA_KCLASS = 512


FEAT_BLOCK = HEAD_DIM
FEAT_PAD = HEAD_DIM + 32


def _nsa_kernel(q_ref, kc_ref, vc_ref, ks_ref, vs_ref, kw_ref, vw_ref, gt_ref, w1_ref, pos_ref, w2_ref, ov_ref,
                o_ref, kcmp_s, vcmp_s, ks_s, vs_s, kw_s, vw_s, *, seq, n_top):
    g = pl.program_id(1)
    qi = pl.program_id(2)
    tq, dh, r_heads = NSA_TQ, HEAD_DIM, Q_PER_KV
    rows = r_heads * tq
    scale = dh ** -0.5
    assert math.frexp(scale)[0] == 0.5
    n_cmp_pad = seq // CMP_STRIDE
    n_slc = seq // SLC_BLOCK
    assert FEAT_BLOCK + n_slc <= FEAT_PAD < LANES
    lane = lax.broadcasted_iota(I32, (1, LANES), 1)

    @pl.when(qi == 0)
    def _prepare():
        def pick(ref):
            xx = ref[...]
            return jnp.where(g == 0, xx[:, :dh], xx[:, dh:]).astype(BF16)

        def low(ref):
            xx = ref[...]
            return jnp.where(g == 0, xx, pltpu.roll(xx, dh, 1))

        blk = lax.broadcasted_iota(I32, (seq, 1), 0) // SLC_BLOCK
        ks_s[...] = jnp.where(lane < dh, low(ks_ref), jnp.where(lane - FEAT_BLOCK == blk, 1.0, 0.0)).astype(BF16)
        vs_s[...] = pick(vs_ref)
        kw_s[0:WIN, :] = jnp.broadcast_to(jnp.where(lane == FEAT_PAD, 1.0, 0.0), (WIN, LANES)).astype(BF16)
        vw_s[0:WIN, :] = jnp.zeros((WIN, dh), BF16)
        kw_s[WIN:, :] = jnp.where(lane < dh, low(kw_ref), 0.0).astype(BF16)
        vw_s[WIN:, :] = pick(vw_ref)
        half = CMP_BLOCK // 2
        for kv, (src, dst) in enumerate(((kc_ref, kcmp_s), (vc_ref, vcmp_s))):
            ya = jnp.zeros((n_cmp_pad, CMP_BLOCK * 4), F32)
            yb = jnp.zeros((n_cmp_pad, CMP_BLOCK * 4), F32)
            for l in range(half):
                xl = src[pl.ds(l, n_cmp_pad, stride=CMP_STRIDE), :]
                ya = ya + _dot((xl + pos_ref[kv, l:l + 1, :]).astype(BF16), w1_ref[kv, l])
                yb = yb + _dot((xl + pos_ref[kv, half + l:half + l + 1, :]).astype(BF16), w1_ref[kv, half + l])
            h1 = ya + pltpu.roll(yb, n_cmp_pad - 1, 0)
            h1 = jax.nn.gelu(h1, approximate=True)
            cmp = _dot(h1.astype(BF16), w2_ref[kv]).astype(BF16)
            if kv == 0:
                cmp = jnp.concatenate([cmp, jnp.zeros((n_cmp_pad, LANES - dh), BF16)], axis=1)
            dst[...] = cmp

    q0 = qi * tq
    r0 = pl.multiple_of(q0, tq)
    q4 = q_ref[...] * scale
    q_lo = []
    for r in range(r_heads):
        qp = q4[:, (r // 2) * LANES:(r // 2 + 1) * LANES]
        q_lo.append(pltpu.roll(qp, dh, 1) if r % 2 else qp)
    q_z = jnp.concatenate([jnp.where(lane < dh, qq, 0.0) for qq in q_lo], axis=0).astype(BF16)
    pos = q0 + (lax.broadcasted_iota(I32, (rows, 1), 0) & (tq - 1))

    s = _dot_nt(q_z, kcmp_s[...])
    n_idx = lax.broadcasted_iota(I32, (1, n_cmp_pad), 1)
    valid = (n_idx * CMP_STRIDE + (CMP_BLOCK - 1)) <= pos
    s = jnp.where(valid, s, NEG_INF)
    s = s - jnp.max(s, axis=-1, keepdims=True)
    p = jnp.exp(s)
    p = p / jnp.sum(p, axis=-1, keepdims=True)
    p = jnp.where(pos >= CMP_BLOCK - 1, p, 0.0).astype(BF16)
    o_cmp = _dot(p, vcmp_s[...])
    imp4 = _dot(p, ov_ref[...])
    imp = imp4[0:tq]
    for r in range(1, r_heads):
        imp = imp + imp4[r * tq:(r + 1) * tq]

    sc = imp.T[0:n_slc, :]
    j_idx = lax.broadcasted_iota(I32, (n_slc, 1), 0)
    cur = (q0 + lax.broadcasted_iota(I32, (1, tq), 1)) // SLC_BLOCK
    forced = (j_idx == 0) | (j_idx == cur) | (j_idx == cur - 1)
    score = jnp.where(j_idx > cur, -jnp.inf, jnp.where(forced, FORCE_SCORE, sc))
    cnt = jnp.zeros((n_slc, tq), F32)
    for i in range(n_slc):
        ri = score[i:i + 1, :]
        beats = (ri > score) | ((ri == score) & (j_idx > i))
        cnt = cnt + jnp.where(beats, 1.0, 0.0)
    blk_bias = jnp.where((cnt < n_top) & (j_idx < q0 // SLC_BLOCK), 0.0, NEG_INF)
    row = lax.broadcasted_iota(I32, (LANES, 1), 0)
    ext_t = jnp.concatenate([jnp.zeros((FEAT_BLOCK, tq), F32), blk_bias,
                             jnp.zeros((LANES - FEAT_BLOCK - n_slc, tq), F32)], axis=0)
    ext = jnp.where(row == FEAT_PAD, NEG_INF, ext_t).T
    q_s = jnp.concatenate([jnp.where(lane < dh, qq, ext) for qq in q_lo], axis=0).astype(BF16)

    kk = lax.broadcasted_iota(I32, (1, tq), 1)
    tt = lax.broadcasted_iota(I32, (tq, 1), 0)
    causal = jnp.concatenate([jnp.where(kk <= tt, 0.0, NEG_INF)] * r_heads, axis=0)
    oldest = jnp.concatenate([jnp.where(kk > tt, 0.0, NEG_INF)] * r_heads, axis=0)

    def attend(parts):
        m = None
        for s_part, _ in parts:
            mp = jnp.max(s_part, axis=-1, keepdims=True)
            m = mp if m is None else jnp.maximum(m, mp)
        l, acc = 0.0, 0.0
        for s_part, v_part in parts:
            p_part = jnp.exp(s_part - m)
            l = l + jnp.sum(p_part, axis=-1, keepdims=True)
            acc = acc + _dot(p_part.astype(BF16), v_part)
        return acc / l

    s_own = _dot_nt(q_z, ks_s[pl.ds(r0, tq), :]) + causal
    v_own = vs_s[pl.ds(r0, tq), :]

    band = WIN + tq
    q_w = jnp.concatenate([jnp.where(lane < dh, qq, jnp.where(lane == FEAT_PAD, NEG_INF, 0.0)) for qq in q_lo],
                          axis=0).astype(BF16)

    def window():
        s_w = _dot_nt(q_w, kw_s[pl.ds(r0, band), :])
        s_w = jnp.concatenate([s_w[:, 0:tq] + oldest, s_w[:, tq:WIN], s_w[:, WIN:] + causal], axis=1)
        return attend([(s_w, vw_s[pl.ds(r0, band), :])])

    gate = jax.nn.sigmoid(gt_ref[...])
    gates = [jnp.broadcast_to(gate[:, c:c + 1], (tq, dh)) for c in range(3 * r_heads)]

    def slc_class(nk):
        return lambda: (attend([(_dot_nt(q_s, ks_s[0:nk, :]), vs_s[0:nk, :]), (s_own, v_own)]), window())

    o_slc, o_win = lax.switch(q0 // NSA_KCLASS, [slc_class(nk) for nk in range(NSA_KCLASS, seq + 1, NSA_KCLASS)])

    outs = []
    for r in range(r_heads):
        sl = slice(r * tq, (r + 1) * tq)
        outs.append(gates[3 * r] * o_cmp[sl] + gates[3 * r + 1] * o_slc[sl] + gates[3 * r + 2] * o_win[sl])
    o_ref[...] = jnp.concatenate(outs, axis=1).astype(o_ref.dtype)


def _nsa_constants(seq):
    n_cmp = (seq - CMP_BLOCK) // CMP_STRIDE + 1
    n_slc = seq // SLC_BLOCK
    cmp_start = np.arange(n_cmp) * CMP_STRIDE
    slc_start = np.arange(n_slc) * SLC_BLOCK
    overlap = ((cmp_start[:, None] < slc_start[None, :] + SLC_BLOCK)
               & (cmp_start[:, None] + CMP_BLOCK > slc_start[None, :])).astype(np.float32)
    ov = np.zeros((seq // CMP_STRIDE, LANES), np.float32)
    ov[:n_cmp, :n_slc] = overlap
    return jnp.asarray(ov, BF16)


def _nsa(h, w1pad, pos2, w2, *, batch, seq):
    t = h.shape[0]
    nq = seq // NSA_TQ
    ov = _nsa_constants(seq)
    n_top = min(SLC_TOPN, seq // SLC_BLOCK)
    kv_spec = lambda c: pl.BlockSpec((seq, LANES), lambda b, g, qi, c=c: (b, c))
    kvb = COL_KV // LANES
    in_specs = [
        pl.BlockSpec((NSA_TQ, Q_PER_KV * HEAD_DIM), lambda b, g, qi: (b * nq + qi, g)),
        kv_spec(kvb), kv_spec(kvb + 1), kv_spec(kvb + 2), kv_spec(kvb + 3), kv_spec(kvb + 4), kv_spec(kvb + 5),
        pl.BlockSpec((NSA_TQ, LANES), lambda b, g, qi: (b * nq + qi, COL_GATE // LANES + g)),
        pl.BlockSpec((2, None, CMP_BLOCK, LANES, LANES), lambda b, g, qi: (0, g, 0, 0, 0)),
        pl.BlockSpec(pos2.shape, lambda b, g, qi: (0, 0, 0)),
        pl.BlockSpec(w2.shape, lambda b, g, qi: (0, 0, 0)),
        pl.BlockSpec(ov.shape, lambda b, g, qi: (0, 0)),
    ]
    n_cmp_pad = seq // CMP_STRIDE
    assert seq % NSA_KCLASS == 0 and NSA_KCLASS % NSA_TQ == 0
    scratch = [pltpu.VMEM((n_cmp_pad, LANES), BF16), pltpu.VMEM((n_cmp_pad, HEAD_DIM), BF16),
               pltpu.VMEM((seq, LANES), BF16), pltpu.VMEM((seq, HEAD_DIM), BF16),
               pltpu.VMEM((seq + WIN, LANES), BF16), pltpu.VMEM((seq + WIN, HEAD_DIM), BF16)]
    return pl.pallas_call(
        functools.partial(_nsa_kernel, seq=seq, n_top=n_top),
        grid=(batch, KV_GROUPS, nq),
        in_specs=in_specs,
        out_specs=pl.BlockSpec((NSA_TQ, Q_PER_KV * HEAD_DIM), lambda b, g, qi: (b * nq + qi, g)),
        out_shape=jax.ShapeDtypeStruct((t, KV_GROUPS * Q_PER_KV * HEAD_DIM), BF16),
        scratch_shapes=scratch,
        compiler_params=_cparams(("arbitrary", "arbitrary", "arbitrary")),
        name="nsa",
    )(h, h, h, h, h, h, h, h, w1pad, pos2, w2, ov)


HG_CHUNKS_PER_STEP = 2


def _boundary_rows(b_ref, m):
    blk = 2 * m
    pieces = []
    sub = lax.broadcasted_iota(I32, (SUBLANES, 1), 0)
    for v in range(HG_CHUNK // SUBLANES):
        if blk >= SUBLANES:
            r = (SUBLANES * v // blk) * blk + m - 1
            pieces.append(jnp.broadcast_to(b_ref[r:r + 1, :], (SUBLANES, HG_D)))
        else:
            acc = None
            for k in range(SUBLANES // blk):
                r = SUBLANES * v + k * blk + m - 1
                rowv = jnp.broadcast_to(b_ref[r:r + 1, :], (SUBLANES, HG_D))
                acc = rowv if acc is None else jnp.where(sub >= k * blk, rowv, acc)
            pieces.append(acc)
    return jnp.concatenate(pieces, axis=0)


def _hgrn_kernel(q_ref, f_ref, i_ref, g_ref, lbl_ref, nw_ref, o_ref, st_ref, b_scr):
    ci = pl.program_id(1)
    c_rows = HG_CHUNK

    @pl.when(ci == 0)
    def _reset():
        st_ref[...] = jnp.zeros_like(st_ref)

    lg = lbl_ref[...]
    e = jnp.exp(lg - jnp.max(lg, axis=0, keepdims=True))
    lb_all = e[0:1, :] / jnp.sum(e, axis=0, keepdims=True)

    row = lax.broadcasted_iota(I32, (c_rows, 1), 0)
    tt = lax.broadcasted_iota(I32, (c_rows, c_rows), 0)
    ss = lax.broadcasted_iota(I32, (c_rows, c_rows), 1)
    for c in range(HG_CHUNKS_PER_STEP):
        for hd in range(HG_HEADS):
            rs = slice(c * c_rows, (c + 1) * c_rows)
            cs = slice(hd * HG_D, (hd + 1) * HG_D)
            q = q_ref[rs, cs]
            v = i_ref[rs, cs]
            gg = g_ref[rs, cs]
            lb = lb_all[:, cs]
            f = lb + (1.0 - lb) * jax.nn.sigmoid(f_ref[rs, cs])
            k = 1.0 - f
            b = jnp.log(f)
            sh = 1
            while sh < c_rows:
                b = b + jnp.where(row >= sh, pltpu.roll(b, sh, 0), 0.0)
                sh *= 2
            b_scr[hd] = b
            bl = b[c_rows - 1:c_rows, :]
            st = st_ref[hd]
            vb = v.astype(BF16)
            o = _dot_nt((q * jnp.exp(b)).astype(BF16), st.astype(BF16))
            a = jnp.where(tt == ss, _dot_nt(q.astype(BF16), k.astype(BF16)), 0.0)
            m = 1
            while m < c_rows:
                ee = jnp.exp(-jnp.abs(b - _boundary_rows(b_scr.at[hd], m)))
                upper = (row & (2 * m - 1)) >= m
                qt = jnp.where(upper, q * ee, 0.0).astype(BF16)
                kt = jnp.where(upper, 0.0, k * ee).astype(BF16)
                shift = int(math.log2(2 * m))
                a = a + jnp.where((tt >> shift) == (ss >> shift), _dot_nt(qt, kt), 0.0)
                m *= 2
            o = o + _dot(a.astype(BF16), vb)
            st_ref[hd] = st * jnp.exp(bl) + _dot_tn(vb, (k * jnp.exp(bl - b)).astype(BF16))
            o = o * lax.rsqrt(jnp.mean(o * o, axis=-1, keepdims=True) + LN_EPS) * nw_ref[:, cs]
            o_ref[rs, cs] = (o * (gg * jax.nn.sigmoid(gg))).astype(o_ref.dtype)


def _hgrn(h, lb_logits, norm_w, *, batch, seq):
    t = h.shape[0]
    rows = HG_CHUNKS_PER_STEP * HG_CHUNK
    steps = seq // rows
    width = HG_HEADS * HG_D
    hb = COL_HG // width
    spec = lambda c: pl.BlockSpec((rows, width), lambda b, ci, c=c: (b * steps + ci, c))
    return pl.pallas_call(
        _hgrn_kernel,
        grid=(batch, steps),
        in_specs=[spec(hb), spec(hb + 1), spec(hb + 2), spec(hb + 3),
                  pl.BlockSpec(lb_logits.shape, lambda b, ci: (0, 0)),
                  pl.BlockSpec(norm_w.shape, lambda b, ci: (0, 0))],
        out_specs=pl.BlockSpec((rows, width), lambda b, ci: (b * steps + ci, 0)),
        out_shape=jax.ShapeDtypeStruct((t, width), BF16),
        scratch_shapes=[pltpu.VMEM((HG_HEADS, HG_D, HG_D), F32), pltpu.VMEM((HG_HEADS, HG_CHUNK, HG_D), F32)],
        compiler_params=_cparams(("arbitrary", "arbitrary")),
        name="hgrn",
    )(h, h, h, h, lb_logits, norm_w)


MID_TM = 256


def _mid_kernel(oa_ref, ob_ref, x_ref, kv_ref, wout_ref, wq_ref, wo_ref, pwq_ref, lng_ref, lnb_ref,
                x2_ref, x2b_ref, qry_ref):
    half = oa_ref.shape[1]
    d = x_ref.shape[1]
    mix = _dot(oa_ref[...], wout_ref[0:half, :]) + _dot(ob_ref[...], wout_ref[half:, :])
    x1 = _layer_norm(ALPHA * x_ref[...] + mix, lng_ref[0:1, :], lnb_ref[0:1, :])
    q = _dot(x1.astype(BF16), wq_ref[...])
    dh = d // XA_HEADS
    outs = []
    for hd in range(XA_HEADS):
        qh = q[:, hd * dh:(hd + 1) * dh].astype(BF16)
        kh = kv_ref[:, hd * dh:(hd + 1) * dh]
        vh = kv_ref[:, d + hd * dh:d + (hd + 1) * dh]
        s = _dot_nt(qh, kh) * dh ** -0.5
        s = s - jnp.max(s, axis=-1, keepdims=True)
        p = jnp.exp(s)
        p = p / jnp.sum(p, axis=-1, keepdims=True)
        outs.append(_dot(p.astype(BF16), vh))
    xa = _dot(jnp.concatenate(outs, axis=1).astype(BF16), wo_ref[...])
    x2 = _layer_norm(ALPHA * x1 + xa, lng_ref[1:2, :], lnb_ref[1:2, :])
    x2_ref[...] = x2
    x2b = x2.astype(BF16)
    x2b_ref[...] = x2b
    qry_ref[...] = _dot(x2b, pwq_ref[...]).astype(BF16)


def _mid(o_a, o_b, x, kvm, w_out, wq, wo, pwq, ln_g, ln_b, *, batch, seq):
    t, d = x.shape
    mem_len = kvm.shape[0] // batch
    steps = seq // MID_TM
    row = lambda b, i: (b * steps + i, 0)
    full = lambda b, i: (0, 0)
    half = o_a.shape[1]
    return pl.pallas_call(
        _mid_kernel,
        grid=(batch, steps),
        in_specs=[pl.BlockSpec((MID_TM, half), row), pl.BlockSpec((MID_TM, half), row), pl.BlockSpec((MID_TM, d), row),
                  pl.BlockSpec((mem_len, 2 * d), lambda b, i: (b, 0)),
                  pl.BlockSpec((d, d), full), pl.BlockSpec((d, d), full), pl.BlockSpec((d, d), full),
                  pl.BlockSpec((d, d), full), pl.BlockSpec(ln_g.shape, full), pl.BlockSpec(ln_b.shape, full)],
        out_specs=[pl.BlockSpec((MID_TM, d), row)] * 3,
        out_shape=[jax.ShapeDtypeStruct((t, d), F32), jax.ShapeDtypeStruct((t, d), BF16),
                   jax.ShapeDtypeStruct((t, d), BF16)],
        compiler_params=_cparams(("arbitrary", "arbitrary")),
        name="mid",
    )(o_a, o_b, x, kvm, w_out, wq, wo, pwq, ln_g, ln_b)


TOPK_TT = 128


def _top_rows(s, n):
    rows = s.shape[0]
    sub = lax.broadcasted_iota(I32, (rows, 1), 0)
    rn = lax.broadcasted_iota(I32, (n, 1), 0)
    vals = jnp.zeros((n, s.shape[1]), F32)
    idxs = jnp.zeros((n, s.shape[1]), I32)
    for a in range(n):
        mx = jnp.max(s, axis=0, keepdims=True)
        ix = jnp.min(jnp.where(s == mx, sub, rows), axis=0, keepdims=True)
        vals = jnp.where(rn == a, mx, vals)
        idxs = jnp.where(rn == a, ix, idxs)
        s = jnp.where(sub == ix, -jnp.inf, s)
    return vals, idxs


def _candidate_groups(s1, s2):
    k = PEER_TOPK
    sub = lax.broadcasted_iota(I32, (SUBLANES, 1), 0)
    groups = []
    for a0 in range(0, k, SUBLANES):
        groups.append((s1[a0:a0 + SUBLANES] + s2[0:1], (sub + a0) * k))
    for b in range(1, SUBLANES):
        n_valid = k // (b + 1)
        vals = jnp.where(sub < n_valid, s1[0:SUBLANES] + s2[b:b + 1], -jnp.inf)
        groups.append((vals, sub * k + b))
    for b0 in range(SUBLANES, k, SUBLANES):
        groups.append((s1[0:1] + s2[b0:b0 + SUBLANES], sub + b0))
    return groups


def _peer_retrieve(q, key_ref, interleave):
    k = PEER_TOPK
    dk = key_ref.shape[2]
    rk = lax.broadcasted_iota(I32, (k, 1), 0)
    big = k * k
    log_k = int(math.log2(k))
    assert 1 << log_k == k == 2 * SUBLANES
    i1_all, i2_all, g_all = [], [], []
    for hd in range(PEER_HEADS):
        tops = []
        for c in range(2):
            col = (2 * hd + c) * dk
            tops.append(_top_rows(_dot_nt(key_ref[2 * hd + c], q[:, col:col + dk]), k))
            interleave(2 * hd + c)
        (s1, i1), (s2, i2) = tops
        groups = _candidate_groups(s1, s2)
        top_s = jnp.zeros((k, q.shape[0]), F32)
        e1 = jnp.zeros((k, q.shape[0]), I32)
        e2 = jnp.zeros((k, q.shape[0]), I32)
        for j in range(k):
            mx = groups[0][0]
            for vals, _ in groups[1:]:
                mx = jnp.maximum(mx, vals)
            mx = jnp.max(mx, axis=0, keepdims=True)
            pm = None
            for vals, posn in groups:
                cand = jnp.where(vals == mx, posn, big)
                pm = cand if pm is None else jnp.minimum(pm, cand)
            pm = jnp.min(pm, axis=0, keepdims=True)
            a_sel = pm >> log_k
            b_sel = pm & (k - 1)
            i1_sel = jnp.sum(jnp.where(rk == a_sel, i1, 0), axis=0, keepdims=True)
            i2_sel = jnp.sum(jnp.where(rk == b_sel, i2, 0), axis=0, keepdims=True)
            top_s = jnp.where(rk == j, mx, top_s)
            e1 = jnp.where(rk == j, i1_sel, e1)
            e2 = jnp.where(rk == j, i2_sel, e2)
            groups = [(jnp.where(posn == pm, -jnp.inf, vals), posn) for vals, posn in groups]
        ex = jnp.exp(top_s - top_s[0:1])
        g_all.append(ex / jnp.sum(ex, axis=0, keepdims=True))
        i1_all.append(e1)
        i2_all.append(e2)
    return jnp.concatenate(i1_all, axis=0).T, jnp.concatenate(i2_all, axis=0).T, jnp.concatenate(g_all, axis=0).T


def _gate_map_tokens(i1_ref, i2_ref, gate_ref, o_ref, t0, t1):
    nk = PEER_NKEYS
    n_sel = i1_ref.shape[1]
    sub = lax.broadcasted_iota(I32, (nk, n_sel), 0)
    for t in range(t0, t1):
        at = jnp.where(sub == i1_ref[t:t + 1, :], 1.0, 0.0).astype(BF16)
        bt = jnp.where(sub == i2_ref[t:t + 1, :], gate_ref[t:t + 1, :], 0.0).astype(BF16)
        o_ref[pl.ds(t, nk, stride=GM_PITCH), :] = _dot_nt(at, bt)


def _peer_select_kernel(q_ref, key_ref, o_ref, i1_s, i2_s, gate_s):
    @pl.when(pl.program_id(0) == 0)
    def _first():
        i1_s[...] = jnp.zeros_like(i1_s)
        i2_s[...] = jnp.zeros_like(i2_s)
        gate_s[...] = jnp.zeros_like(gate_s)

    nk = PEER_NKEYS
    for r in range(GM_PITCH - GM_TILE):
        o_ref[pl.ds(GM_TILE + r, nk, stride=GM_PITCH), :] = jnp.zeros((nk, nk), F32)
    per_seg = GM_TILE // (2 * PEER_HEADS)

    def gate_map_segment(n):
        _gate_map_tokens(i1_s, i2_s, gate_s, o_ref, n * per_seg, (n + 1) * per_seg)

    i1, i2, gate = _peer_retrieve(q_ref[...], key_ref, gate_map_segment)
    i1_s[...] = i1
    i2_s[...] = i2
    gate_s[...] = gate


def _peer_select(qry, keys):
    t, d = qry.shape
    n_sel = PEER_HEADS * PEER_TOPK
    n_tiles = t // GM_TILE
    rows_out = PEER_NKEYS * GM_PITCH
    return pl.pallas_call(
        _peer_select_kernel,
        grid=(n_tiles + 1,),
        in_specs=[pl.BlockSpec((GM_TILE, d), lambda i: (jnp.minimum(i, n_tiles - 1), 0)),
                  pl.BlockSpec(keys.shape, lambda i: (0, 0, 0))],
        out_specs=pl.BlockSpec((rows_out, PEER_NKEYS), lambda i: (jnp.maximum(i - 1, 0), 0)),
        out_shape=jax.ShapeDtypeStruct((n_tiles * rows_out, PEER_NKEYS), F32),
        scratch_shapes=[pltpu.VMEM((GM_TILE, n_sel), I32), pltpu.VMEM((GM_TILE, n_sel), I32),
                        pltpu.VMEM((GM_TILE, n_sel), F32)],
        compiler_params=_cparams(("arbitrary",)),
        name="peer_select",
    )(qry, keys)


DENSE_TT = 512
DENSE_NI = 16


def _peer_dense_kernel(xb_ref, x_ref, dn_ref, up_ref, gm_ref, lng_ref, lnb_ref, o_ref, acc_ref):
    j = pl.program_id(1)

    @pl.when(j == 0)
    def _zero():
        acc_ref[...] = jnp.zeros_like(acc_ref)

    hid = _dot_nt(xb_ref[...], dn_ref[...])
    act = 0.5 * hid * (1.0 + lax.erf(hid * (0.5 ** 0.5)))
    gm = jnp.concatenate(
        [jnp.concatenate([gm_ref[qq, ii, 0:GM_TILE, :] for qq in range(DENSE_TT // GM_TILE)], axis=0)
         for ii in range(DENSE_NI)], axis=1)
    acc_ref[...] += _dot((act * gm).astype(BF16), up_ref[...])

    @pl.when(j == pl.num_programs(1) - 1)
    def _finish():
        o_ref[...] = _layer_norm(ALPHA * x_ref[...] + acc_ref[...], lng_ref[2:3, :], lnb_ref[2:3, :])


def _peer_dense(xb, x, down, up, gmap, ln_g, ln_b):
    t, d = x.shape
    n_exp = down.shape[0]
    blk = DENSE_NI * PEER_NKEYS
    row = lambda i, j: (i, 0)
    full = lambda i, j: (0, 0)
    return pl.pallas_call(
        _peer_dense_kernel,
        grid=(t // DENSE_TT, n_exp // blk),
        in_specs=[pl.BlockSpec((DENSE_TT, d), row), pl.BlockSpec((DENSE_TT, d), row),
                  pl.BlockSpec((blk, d), lambda i, j: (j, 0)), pl.BlockSpec((blk, d), lambda i, j: (j, 0)),
                  pl.BlockSpec((DENSE_TT // GM_TILE, DENSE_NI, GM_PITCH, PEER_NKEYS), lambda i, j: (i, j, 0, 0)),
                  pl.BlockSpec(ln_g.shape, full), pl.BlockSpec(ln_b.shape, full)],
        out_specs=pl.BlockSpec((DENSE_TT, d), row),
        out_shape=jax.ShapeDtypeStruct((t, d), F32),
        scratch_shapes=[pltpu.VMEM((DENSE_TT, d), F32)],
        compiler_params=_cparams(("arbitrary", "arbitrary")),
        name="peer_dense",
    )(xb, x, down, up, gmap, ln_g, ln_b)


def _pack_w_in(w):
    nsa_w = KV_GROUPS * Q_PER_KV * HEAD_DIM
    kv_w = 6 * KV_GROUPS * HEAD_DIM
    n_gate = Q_PER_KV * 3
    gates = w[:, nsa_w + kv_w:nsa_w + kv_w + KV_GROUPS * n_gate]
    pad = jnp.zeros((w.shape[0], LANES - n_gate), w.dtype)
    parts = [w[:, :nsa_w + kv_w]]
    for g in range(KV_GROUPS):
        parts += [gates[:, g * n_gate:(g + 1) * n_gate], pad]
    parts.append(w[:, nsa_w + kv_w + KV_GROUPS * n_gate:])
    return jnp.concatenate(parts, axis=1).astype(BF16)


def _pad_cmp_w1(w1):
    w1r = w1.reshape(2, CMP_BLOCK, HEAD_DIM, w1.shape[-1]).astype(BF16)
    z = jnp.zeros_like(w1r)
    return jnp.stack([jnp.concatenate([w1r, z], axis=2), jnp.concatenate([z, w1r], axis=2)], axis=1)


def kernel(x, mem, w_in, cmp_pos, cmp_w1, cmp_w2, hgrn_lb_logits, hgrn_norm_w, w_out, xa_wq, xa_wkv, xa_wo, peer_wq, peer_subkeys, peer_down, peer_up, ln_g, ln_b):
    batch, seq, d = x.shape
    t = batch * seq
    assert w_in.shape[0] == DEPTH and w_in.shape[2] + 2 * LANES - KV_GROUPS * Q_PER_KV * 3 == PACKED_COLS
    x2d = x.reshape(t, d)

    h = _matmul(x2d, _pack_w_in(w_in[0]), F32, 256)
    pos2 = jnp.concatenate([cmp_pos[0]] * KV_GROUPS, axis=-1)
    o_a = _nsa(h, _pad_cmp_w1(cmp_w1[0]), pos2, cmp_w2[0].astype(BF16), batch=batch, seq=seq)
    o_b = _hgrn(h, hgrn_lb_logits.reshape(hgrn_lb_logits.shape[0], HG_HEADS * HG_D),
                hgrn_norm_w[0].reshape(1, HG_HEADS * HG_D), batch=batch, seq=seq)

    kvm = _matmul(mem.reshape(batch * mem.shape[1], d), xa_wkv[0].astype(BF16), BF16, 256)
    x2, x2b, qry = _mid(o_a, o_b, x2d, kvm, w_out[0].astype(BF16), xa_wq[0].astype(BF16), xa_wo[0].astype(BF16),
                        peer_wq[0].astype(BF16), ln_g[0], ln_b[0], batch=batch, seq=seq)

    keys = peer_subkeys[0].reshape(PEER_HEADS * 2, PEER_NKEYS, peer_subkeys.shape[-1]).astype(BF16)
    gmap = _peer_select(qry, keys).reshape(t // GM_TILE, PEER_NKEYS, GM_PITCH, PEER_NKEYS)
    out = _peer_dense(x2b, x2, peer_down[0].astype(BF16), peer_up[0].astype(BF16), gmap, ln_g[0], ln_b[0])
    return out.reshape(batch, seq, d)
```
